```python
import math
import jax, jax.numpy as jnp
from jax import lax
import numpy as np

D_MODEL = 2048
BATCH = 4
SEQ = 4096
DEPTH = 4

GRID_W = 64
CTX_LEN = 256
NORM_EPS = 1e-6
N_EVEN = (DEPTH + 1) // 2
N_ODD = DEPTH // 2
MIX_WIDTH = D_MODEL
SSD_INNER = MIX_WIDTH // 2
SSD_HEAD_DIM = 64
SSD_HEADS = SSD_INNER // SSD_HEAD_DIM
SSD_GROUPS = 2
SSD_HPG = SSD_HEADS // SSD_GROUPS
SSD_STATE = 128
SSD_CHUNK = 128
CONV_K = 5
XBC_DIM = SSD_INNER + 2 * SSD_GROUPS * SSD_STATE
SGU_WIDTH = MIX_WIDTH - SSD_INNER
SGU_CHUNK = 128
SGU_GROUP_DIM = 128
SGU_GROUPS = SGU_WIDTH // SGU_GROUP_DIM
Z_END = SSD_INNER
XBC_END = Z_END + XBC_DIM
DT_END = XBC_END + 2 * SSD_HEADS
U_END = DT_END + SGU_WIDTH
EVEN_IN_COLS = U_END + SGU_WIDTH
ATT_HEAD_DIM = 128
ATT_HEADS = MIX_WIDTH // ATT_HEAD_DIM
ATT_KV_HEADS = ATT_HEADS // 4
ATT_GROUP = ATT_HEADS // ATT_KV_HEADS
Q_COLS = ATT_HEADS * ATT_HEAD_DIM
KV_COLS = ATT_KV_HEADS * ATT_HEAD_DIM
QKV_COLS = Q_COLS + 2 * KV_COLS
WINDOW = 128
ATT_BLOCK = 128
ROPE_BASE = 10000.0
NEG_INF = -1e30
FFN_HIDDEN = -(-(8 * D_MODEL) // (3 * 256)) * 256

kernel_name = 'hybrid_ssd_sgu_swa_diffusion_trunk'


def rms_norm(x, w):
    xf = x.astype(jnp.float32)
    y = xf * lax.rsqrt(jnp.mean(xf * xf, axis=-1, keepdims=True) + NORM_EPS)
    return (y * w.astype(jnp.float32)).astype(x.dtype)


def swiglu(h, w_in, w_out):
    g, u = jnp.split(h @ w_in, 2, axis=-1)
    return (jax.nn.silu(g) * u) @ w_out


def dwconv_centred(x, w, b):
    y = lax.conv_general_dilated(x, w.astype(x.dtype)[:, None, :], window_strides=(1,),
                                 padding=[(CONV_K // 2, CONV_K // 2)],
                                 dimension_numbers=('NWC', 'WIO', 'NWC'),
                                 feature_group_count=x.shape[-1])
    return y + b.astype(x.dtype)


def axial_rope_tables(S):
    rows = S // GRID_W
    row = jnp.repeat(jnp.arange(rows, dtype=jnp.float32), GRID_W)
    col = jnp.tile(jnp.arange(GRID_W, dtype=jnp.float32), rows)
    quarter = ATT_HEAD_DIM // 4
    inv = ROPE_BASE ** (-jnp.arange(quarter, dtype=jnp.float32) / quarter)
    ang = jnp.stack([row[:, None] * inv, col[:, None] * inv], axis=1)
    return jnp.cos(ang), jnp.sin(ang)


def axial_rope(x, cos, sin):
    b, S, H, d = x.shape
    xs = x.reshape(b, S, H, 2, 2, d // 4)
    x1, x2 = xs[..., 0, :], xs[..., 1, :]
    cs = cos[:, None].astype(x.dtype)
    sn = sin[:, None].astype(x.dtype)
    return jnp.stack([x1 * cs - x2 * sn, x2 * cs + x1 * sn], axis=-2).reshape(b, S, H, d)


def ssd_chunked(xh, dt, a, bm, cm, init_state, need_y):
    b, L, G, J, P = xh.shape
    N = bm.shape[-1]
    Q = SSD_CHUNK
    nc = L // Q
    X = (xh * dt[..., None]).reshape(b, nc, Q, G, J, P)
    Bc = bm.reshape(b, nc, Q, G, N)
    Cc = cm.reshape(b, nc, Q, G, N)
    a_cs = jnp.cumsum((dt * a).reshape(b, nc, Q, G, J), axis=2)
    a_last = a_cs[:, :, -1]
    states = jnp.einsum('bcsgn,bcsgjp->bcgjpn', Bc, X * jnp.exp(a_last[:, :, None] - a_cs)[..., None])

    def step(s, inp):
        decay, st = inp
        return s * decay[..., None, None] + st, s

    final, s_prev = lax.scan(step, init_state, (jnp.moveaxis(jnp.exp(a_last), 1, 0), jnp.moveaxis(states, 1, 0)))
    if not need_y:
        return None, final
    s_prev = jnp.moveaxis(s_prev, 0, 1)
    seg = a_cs[:, :, :, None] - a_cs[:, :, None]
    lower = jnp.tril(jnp.ones((Q, Q), bool))[None, None, :, :, None, None]
    decay_ls = jnp.where(lower, jnp.exp(jnp.where(lower, seg, 0.0)), 0.0)
    cb = jnp.einsum('bclgn,bcsgn->bclsg', Cc, Bc)
    y_diag = jnp.einsum('bclsgj,bcsgjp->bclgjp', cb[..., None] * decay_ls, X)
    y_off = jnp.einsum('bclgn,bcgjpn->bclgjp', Cc, s_prev) * jnp.exp(a_cs)[..., None]
    return (y_diag + y_off).reshape(b, L, G, J, P), final


def ssd_inputs(p, conv_w, conv_b, dt_bias):
    b, L, _ = p.shape
    xbc = jax.nn.silu(dwconv_centred(p[..., :XBC_DIM], conv_w, conv_b)).astype(jnp.float32)
    gn = SSD_GROUPS * SSD_STATE
    xs = xbc[..., :SSD_INNER].reshape(b, L, SSD_GROUPS, SSD_HPG, SSD_HEAD_DIM)
    bm = xbc[..., SSD_INNER:SSD_INNER + gn].reshape(b, L, SSD_GROUPS, SSD_STATE)
    cm = xbc[..., SSD_INNER + gn:].reshape(b, L, SSD_GROUPS, SSD_STATE)
    dt = jax.nn.softplus(p[..., XBC_DIM:].astype(jnp.float32).reshape(b, L, 2, SSD_GROUPS, SSD_HPG)
                         + dt_bias.astype(jnp.float32).reshape(2, SSD_GROUPS, SSD_HPG))
    return xs, bm, cm, dt


def ssd_bidirectional(ctx_in, lat_in, a_log, need_ctx):
    xc, bc, cc, dtc = ctx_in
    xl, bl, cl, dtl = lat_in
    b = xl.shape[0]
    y_ctx, y_lat = None, None
    for d in range(2):
        a = -jnp.exp(a_log[d].astype(jnp.float32)).reshape(SSD_GROUPS, SSD_HPG)
        fl = (lambda t: jnp.flip(t, axis=1)) if d == 1 else (lambda t: t)
        init = jnp.zeros((b, SSD_GROUPS, SSD_HPG, SSD_HEAD_DIM, SSD_STATE), jnp.float32)
        yc, s_ctx = ssd_chunked(fl(xc), fl(dtc[:, :, d]), a, fl(bc), fl(cc), init, need_ctx)
        yl, _ = ssd_chunked(fl(xl), fl(dtl[:, :, d]), a, fl(bl), fl(cl), s_ctx, True)
        y_lat = fl(yl) if y_lat is None else y_lat + fl(yl)
        if need_ctx:
            y_ctx = fl(yc) if y_ctx is None else y_ctx + fl(yc)
    return y_ctx, y_lat


def ssd_finish(y, xs, z, d_skip, norm_w):
    b, L = y.shape[:2]
    y = y + xs * d_skip.astype(jnp.float32).reshape(SSD_GROUPS, SSD_HPG)[..., None]
    y = y.reshape(b, L, SSD_INNER) * jax.nn.silu(z.astype(jnp.float32))
    yg = y.reshape(b, L, SSD_GROUPS, SSD_INNER // SSD_GROUPS)
    yg = yg * lax.rsqrt(jnp.mean(yg * yg, axis=-1, keepdims=True) + NORM_EPS)
    return (yg.reshape(b, L, SSD_INNER) * norm_w.astype(jnp.float32)).astype(z.dtype)


def spatial_gating(u, v, sgu_w, sgu_b):
    b, L, _ = u.shape
    u = jax.nn.gelu(u)
    vf = jax.nn.gelu(v).astype(jnp.float32)
    mu = jnp.mean(vf, axis=-1, keepdims=True)
    var = jnp.mean(jnp.square(vf - mu), axis=-1, keepdims=True)
    vn = ((vf - mu) * lax.rsqrt(var + NORM_EPS)).astype(v.dtype)
    vc = vn.reshape(b, L // SGU_CHUNK, SGU_CHUNK, SGU_GROUPS, SGU_GROUP_DIM)
    mixed = jnp.einsum('gij,bnjgc->bnigc', sgu_w.astype(v.dtype), vc) + sgu_b.astype(v.dtype).T[None, None, :, :, None]
    return u * mixed.reshape(b, L, SGU_WIDTH)


def ssd_sgu_mixer(h_ctx, h_lat, w_in, conv_w, conv_b, dt_bias, a_log, d_skip, ssd_norm_w, sgu_w, sgu_b, w_out, need_ctx):
    p_lat = h_lat @ w_in
    if need_ctx:
        p_ctx = h_ctx @ w_in
        ctx_ssd_cols = p_ctx[..., Z_END:DT_END]
    else:
        p_ctx = None
        ctx_ssd_cols = h_ctx @ w_in[:, Z_END:DT_END]
    lat_in = ssd_inputs(p_lat[..., Z_END:DT_END], conv_w, conv_b, dt_bias)
    ctx_in = ssd_inputs(ctx_ssd_cols, conv_w, conv_b, dt_bias)
    y_ctx, y_lat = ssd_bidirectional(ctx_in, lat_in, a_log, need_ctx)

    def merge(p, y, xs):
        y_ssd = ssd_finish(y, xs, p[..., :Z_END], d_skip, ssd_norm_w)
        y_sgu = spatial_gating(p[..., DT_END:U_END], p[..., U_END:], sgu_w, sgu_b)
        return jnp.concatenate([y_ssd, y_sgu], axis=-1) @ w_out

    o_lat = merge(p_lat, y_lat, lat_in[0])
    o_ctx = merge(p_ctx, y_ctx, ctx_in[0]) if need_ctx else None
    return o_ctx, o_lat


def banded_window_attention(q, k, v, k_c, v_c, sink_kg):
    b, S, H, Dh = q.shape
    T = k_c.shape[1]
    nb = S // ATT_BLOCK
    nw = 3 * ATT_BLOCK
    qb = q.reshape(b, nb, ATT_BLOCK, ATT_KV_HEADS, ATT_GROUP, Dh)
    pad = ((0, 0), (ATT_BLOCK, ATT_BLOCK), (0, 0), (0, 0))

    def band(t):
        tp = jnp.pad(t, pad).reshape(b, nb + 2, ATT_BLOCK, ATT_KV_HEADS, Dh)
        return jnp.concatenate([tp[:, :-2], tp[:, 1:-1], tp[:, 2:]], axis=2)

    kb, vb = band(k), band(v)
    s_win = jnp.einsum('bnqkgd,bnskd->bnkgqs', qb, kb).astype(jnp.float32)
    qpos = (jnp.arange(nb)[:, None] * ATT_BLOCK + jnp.arange(ATT_BLOCK)[None, :])[:, :, None]
    kpos = ((jnp.arange(nb)[:, None] - 1) * ATT_BLOCK + jnp.arange(nw)[None, :])[:, None, :]
    valid = (jnp.abs(qpos - kpos) <= WINDOW) & (kpos >= 0) & (kpos < S)
    s_win = jnp.where(valid[None, :, None, None], s_win, NEG_INF)
    s_ctx = jnp.einsum('bnqkgd,btkd->bnkgqt', qb, k_c).astype(jnp.float32)
    sink = jnp.broadcast_to(sink_kg[None, None, :, :, None, None], s_win.shape[:-1] + (1,))
    p = jax.nn.softmax(jnp.concatenate([s_win, s_ctx, sink], axis=-1), axis=-1).astype(v.dtype)
    o = (jnp.einsum('bnkgqs,bnskd->bnqkgd', p[..., :nw], vb)
         + jnp.einsum('bnkgqt,btkd->bnqkgd', p[..., nw:nw + T], v_c))
    return o.reshape(b, S, H * Dh)


def context_attention(q_c, k_c, v_c, sink_kg):
    b, T = q_c.shape[:2]
    s = jnp.einsum('btkgd,bukd->bkgtu', q_c, k_c).astype(jnp.float32)
    sink = jnp.broadcast_to(sink_kg[None, :, :, None, None], s.shape[:-1] + (1,))
    p = jax.nn.softmax(jnp.concatenate([s, sink], axis=-1), axis=-1)[..., :-1].astype(v_c.dtype)
    return jnp.einsum('bkgtu,bukd->btkgd', p, v_c).reshape(b, T, ATT_HEADS * ATT_HEAD_DIM)


def window_gqa_mixer(h_ctx, h_lat, w_qkv, sink, w_out, cos, sin, need_ctx):
    b, S, _ = h_lat.shape
    T = h_ctx.shape[1]
    scale = ATT_HEAD_DIM ** -0.5
    p_lat = h_lat @ w_qkv
    q_l = axial_rope(p_lat[..., :Q_COLS].reshape(b, S, ATT_HEADS, ATT_HEAD_DIM), cos, sin) * scale
    k_l = axial_rope(p_lat[..., Q_COLS:Q_COLS + KV_COLS].reshape(b, S, ATT_KV_HEADS, ATT_HEAD_DIM), cos, sin)
    v_l = p_lat[..., Q_COLS + KV_COLS:].reshape(b, S, ATT_KV_HEADS, ATT_HEAD_DIM)
    p_ctx = h_ctx @ w_qkv if need_ctx else h_ctx @ w_qkv[:, Q_COLS:]
    k_c = p_ctx[..., -2 * KV_COLS:-KV_COLS].reshape(b, T, ATT_KV_HEADS, ATT_HEAD_DIM)
    v_c = p_ctx[..., -KV_COLS:].reshape(b, T, ATT_KV_HEADS, ATT_HEAD_DIM)
    sink_kg = sink.astype(jnp.float32).reshape(ATT_KV_HEADS, ATT_GROUP)
    o_lat = banded_window_attention(q_l, k_l, v_l, k_c, v_c, sink_kg) @ w_out
    o_ctx = None
    if need_ctx:
        q_c = p_ctx[..., :Q_COLS].reshape(b, T, ATT_KV_HEADS, ATT_GROUP, ATT_HEAD_DIM) * scale
        o_ctx = context_attention(q_c, k_c, v_c, sink_kg) @ w_out
    return o_ctx, o_lat


def setup_inputs(seed: int = 0) -> dict:
    key = jax.random.key(seed)
    ks = jax.random.split(key, 24)
    D = D_MODEL

    def nrm(k, shape, scale):
        return jax.random.normal(k, shape, jnp.float32) * scale

    dt0 = jnp.exp(jax.random.uniform(ks[12], (N_EVEN, 2, SSD_HEADS), jnp.float32,
                                     minval=math.log(1e-3), maxval=math.log(1e-1)))
    return {
        'x': nrm(ks[0], (BATCH, SEQ, D), 1.0),
        'c': nrm(ks[1], (BATCH, D), 1.0),
        'ctx': nrm(ks[2], (BATCH, CTX_LEN, D), 1.0),
        'c_ctx': nrm(ks[3], (D,), 1.0),
        'w_mod': nrm(ks[4], (DEPTH, D, 6 * D), 0.5 * D ** -0.5),
        'b_mod': nrm(ks[5], (DEPTH, 6 * D), 0.02),
        'norm_w': 1.0 + nrm(ks[6], (DEPTH, 4, D), 0.05),
        'w_ffn_in': nrm(ks[7], (DEPTH, D, 2 * FFN_HIDDEN), D ** -0.5),
        'w_ffn_out': nrm(ks[8], (DEPTH, FFN_HIDDEN, D), FFN_HIDDEN ** -0.5),
        'e_w_in': nrm(ks[9], (N_EVEN, D, EVEN_IN_COLS), D ** -0.5),
        'e_conv_w': nrm(ks[10], (N_EVEN, CONV_K, XBC_DIM), CONV_K ** -0.5),
        'e_conv_b': nrm(ks[11], (N_EVEN, XBC_DIM), 0.02),
        'e_dt_bias': dt0 + jnp.log(-jnp.expm1(-dt0)),
        'e_a_log': jnp.log(jax.random.uniform(ks[13], (N_EVEN, 2, SSD_HEADS), jnp.float32, minval=1.0, maxval=16.0)),
        'e_d_skip': 1.0 + nrm(ks[14], (N_EVEN, SSD_HEADS), 0.05),
        'e_ssd_norm_w': 1.0 + nrm(ks[15], (N_EVEN, SSD_INNER), 0.05),
        'e_sgu_w': nrm(ks[16], (N_EVEN, SGU_GROUPS, SGU_CHUNK, SGU_CHUNK), SGU_CHUNK ** -0.5),
        'e_sgu_b': 1.0 + nrm(ks[17], (N_EVEN, SGU_GROUPS, SGU_CHUNK), 0.05),
        'e_w_out': nrm(ks[18], (N_EVEN, MIX_WIDTH, D), MIX_WIDTH ** -0.5),
        'o_w_qkv': nrm(ks[19], (N_ODD, D, QKV_COLS), D ** -0.5),
        'o_sink': nrm(ks[20], (N_ODD, ATT_HEADS), 0.5),
        'o_w_out': nrm(ks[21], (N_ODD, Q_COLS, D), Q_COLS ** -0.5),
    }


def reference(x, c, ctx, c_ctx, w_mod, b_mod, norm_w, w_ffn_in, w_ffn_out, e_w_in, e_conv_w, e_conv_b,
              e_dt_bias, e_a_log, e_d_skip, e_ssd_norm_w, e_sgu_w, e_sgu_b, e_w_out, o_w_qkv, o_sink, o_w_out):
    S = x.shape[1]
    cos, sin = axial_rope_tables(S)
    s_c = jax.nn.silu(c)
    s_cc = jax.nn.silu(c_ctx)
    for l in range(DEPTH):
        need_ctx = l < DEPTH - 1
        mod_l = (s_c @ w_mod[l] + b_mod[l])[:, None, :]
        mod_c = s_cc @ w_mod[l] + b_mod[l]
        sh1, sc1, g1, sh2, sc2, g2 = jnp.split(mod_l, 6, axis=-1)
        csh1, csc1, cg1, csh2, csc2, cg2 = jnp.split(mod_c, 6, axis=-1)
        h_lat = rms_norm(x, norm_w[l, 0]) * (1.0 + sc1) + sh1
        h_ctx = rms_norm(ctx, norm_w[l, 0]) * (1.0 + csc1) + csh1
        if l % 2 == 0:
            i = l // 2
            o_ctx, o_lat = ssd_sgu_mixer(h_ctx, h_lat, e_w_in[i], e_conv_w[i], e_conv_b[i], e_dt_bias[i], e_a_log[i],
                                         e_d_skip[i], e_ssd_norm_w[i], e_sgu_w[i], e_sgu_b[i], e_w_out[i], need_ctx)
        else:
            i = l // 2
            o_ctx, o_lat = window_gqa_mixer(h_ctx, h_lat, o_w_qkv[i], o_sink[i], o_w_out[i], cos, sin, need_ctx)
        x = x + g1 * rms_norm(o_lat, norm_w[l, 1])
        f_lat = swiglu(rms_norm(x, norm_w[l, 2]) * (1.0 + sc2) + sh2, w_ffn_in[l], w_ffn_out[l])
        x = x + g2 * rms_norm(f_lat, norm_w[l, 3])
        if need_ctx:
            ctx = ctx + cg1 * rms_norm(o_ctx, norm_w[l, 1])
            f_ctx = swiglu(rms_norm(ctx, norm_w[l, 2]) * (1.0 + csc2) + csh2, w_ffn_in[l], w_ffn_out[l])
            ctx = ctx + cg2 * rms_norm(f_ctx, norm_w[l, 3])
    return x
```

```python
import functools
import math

import jax
import jax.numpy as jnp
from jax import lax
from jax.experimental import pallas as pl
from jax.experimental.pallas import tpu as pltpu

F32 = jnp.float32
BF16 = jnp.bfloat16

NORM_EPS = 1e-6
NEG_INF = -1e30
GRID_W = 64
ROPE_BASE = 10000.0
SSD_HEAD_DIM = 64
SSD_GROUPS = 2
SSD_STATE = 128
ATT_HEAD_DIM = 128
ATT_GROUP = 4
CHUNK = 128
LANES = 128
DT_PAD = LANES
HALO = 16
VMEM_LIMIT_BYTES = 56 * 1024 * 1024


def _pick(n, pref):
    best = None
    for t in range(LANES, min(n, pref) + 1, LANES):
        if n % t == 0:
            best = t
    assert best is not None, (n, pref)
    return best


def _params(sem):
    return pltpu.CompilerParams(dimension_semantics=sem, vmem_limit_bytes=VMEM_LIMIT_BYTES)


def _rms(x, w):
    return x * lax.rsqrt(jnp.mean(x * x, axis=-1, keepdims=True) + NORM_EPS) * w


def _silu(x):
    return x * jax.nn.sigmoid(x)


def _gelu_tanh(x):
    return 0.5 * x * (1.0 + jnp.tanh(math.sqrt(2.0 / math.pi) * (x + 0.044715 * (x * x * x))))


def _mod_kernel(c_ref, w_ref, b_ref, o_ref):
    s = _silu(c_ref[...]).astype(BF16)
    o_ref[0] = jnp.dot(s, w_ref[0].astype(BF16), preferred_element_type=F32) + b_ref[0]


def _modulation(cond, w_mod, b_mod):
    depth, d, n = w_mod.shape
    rows = cond.shape[0]
    tn = _pick(n, 1024)
    return pl.pallas_call(
        _mod_kernel,
        grid=(depth, n // tn),
        in_specs=[
            pl.BlockSpec((rows, d), lambda l, j: (0, 0)),
            pl.BlockSpec((1, d, tn), lambda l, j: (l, 0, j)),
            pl.BlockSpec((1, 1, tn), lambda l, j: (l, 0, j)),
        ],
        out_specs=pl.BlockSpec((1, rows, tn), lambda l, j: (l, 0, j)),
        out_shape=jax.ShapeDtypeStruct((depth, rows, n), F32),
        compiler_params=_params(("arbitrary", "arbitrary")),
    )(cond, w_mod, b_mod.reshape(depth, 1, n))


def _norm_mod_kernel(x_ref, nw_ref, sc_ref, sh_ref, o_ref):
    o_ref[...] = (_rms(x_ref[...], nw_ref[...]) * (1.0 + sc_ref[...]) + sh_ref[...]).astype(BF16)


def _mm_kernel(a_ref, w_ref, o_ref):
    o_ref[...] = jnp.dot(a_ref[...], w_ref[...], preferred_element_type=F32).astype(o_ref.dtype)


def _mm_swiglu_kernel(a_ref, wg_ref, wu_ref, o_ref):
    a = a_ref[...]
    g = jnp.dot(a, wg_ref[...], preferred_element_type=F32)
    u = jnp.dot(a, wu_ref[...], preferred_element_type=F32)
    o_ref[...] = (_silu(g) * u).astype(o_ref.dtype)


def _mm_softplus_kernel(a_ref, w_ref, b_ref, o_ref):
    p = jnp.dot(a_ref[...], w_ref[...], preferred_element_type=F32) + b_ref[...]
    o_ref[...] = jnp.maximum(p, 0.0) + jnp.log1p(jnp.exp(-jnp.abs(p)))


def _mm_rope_kernel(a_ref, w_ref, cos_ref, sin_ref, o_ref, *, q_blocks, rope_blocks, scale):
    j = pl.program_id(1)
    acc = jnp.dot(a_ref[...], w_ref[...], preferred_element_type=F32)

    @pl.when(j < rope_blocks)
    def _():
        cos = cos_ref[...]
        sin = sin_ref[...]
        lane = lax.broadcasted_iota(jnp.int32, cos.shape, 1)
        first_half = (lane % 64) < 32
        s = jnp.where(j < q_blocks, scale, 1.0).astype(F32)
        for c in range(acc.shape[1] // ATT_HEAD_DIM):
            xc = acc[:, c * ATT_HEAD_DIM:(c + 1) * ATT_HEAD_DIM]
            partner = jnp.where(first_half, pltpu.roll(xc, 96, 1), pltpu.roll(xc, 32, 1))
            o_ref[:, c * ATT_HEAD_DIM:(c + 1) * ATT_HEAD_DIM] = ((xc * cos + partner * sin) * s).astype(o_ref.dtype)

    @pl.when(j >= rope_blocks)
    def _():
        o_ref[...] = acc.astype(o_ref.dtype)


def _resid_epilogue(o, x_ref, g_ref, nw_ref, nxt, xo_ref, ho_ref):
    xn = x_ref[...] + g_ref[...] * _rms(o, nw_ref[...])
    xo_ref[...] = xn
    if ho_ref is not None:
        nwn_ref, sc_ref, sh_ref = nxt
        ho_ref[...] = (_rms(xn, nwn_ref[...]) * (1.0 + sc_ref[...]) + sh_ref[...]).astype(BF16)


def _mm_resid_kernel(*refs, n_a, nk, has_next):
    a_refs = refs[:n_a]
    w_ref, x_ref, g_ref, nw_ref = refs[n_a:n_a + 4]
    pos = n_a + 4
    nxt = refs[pos:pos + 3] if has_next else None
    pos += 3 if has_next else 0
    xo_ref = refs[pos]
    ho_ref = refs[pos + 1] if has_next else None
    acc_ref = refs[-1]
    k = pl.program_id(1)

    def contrib():
        if n_a == 1:
            return jnp.dot(a_refs[0][...], w_ref[...], preferred_element_type=F32)
        a = a_refs[0][...]
        for idx in range(1, n_a):
            a = jnp.where(k == idx, a_refs[idx][...], a)
        return jnp.dot(a, w_ref[...], preferred_element_type=F32)

    if nk == 1:
        _resid_epilogue(contrib(), x_ref, g_ref, nw_ref, nxt, xo_ref, ho_ref)
        return

    @pl.when(k == 0)
    def _():
        acc_ref[...] = contrib()

    @pl.when(jnp.logical_and(k > 0, k < nk - 1))
    def _():
        acc_ref[...] += contrib()

    @pl.when(k == nk - 1)
    def _():
        _resid_epilogue(acc_ref[...] + contrib(), x_ref, g_ref, nw_ref, nxt, xo_ref, ho_ref)


def _conv_kernel(prev_ref, cur_ref, next_ref, w_ref, b_ref, o_ref, ext_ref, *, rows, taps, ctx_blocks,
                 ctx_per_seq, lat_per_seq):
    i = pl.program_id(0)
    in_ctx = i < ctx_blocks
    local = jnp.where(in_ctx, i % ctx_per_seq, (i - ctx_blocks) % lat_per_seq)
    per_seq = jnp.where(in_ctx, ctx_per_seq, lat_per_seq)
    first = local == 0
    last = local == per_seq - 1
    ext_ref[0:HALO, :] = jnp.where(first, 0.0, prev_ref[...].astype(F32))
    ext_ref[HALO:HALO + rows, :] = cur_ref[...].astype(F32)
    ext_ref[HALO + rows:HALO + rows + HALO, :] = jnp.where(last, 0.0, next_ref[...].astype(F32))
    acc = jnp.zeros(o_ref.shape, F32) + b_ref[...]
    for k in range(taps):
        acc = acc + w_ref[k:k + 1, :] * ext_ref[pl.ds(HALO - taps // 2 + k, rows), :]
    o_ref[...] = _silu(acc).astype(o_ref.dtype)


def _cumsum_rows(x):
    rows = x.shape[0]
    row = lax.broadcasted_iota(jnp.int32, x.shape, 0)
    k = 1
    while k < rows:
        x = x + jnp.where(row >= k, pltpu.roll(x, k, 0), 0.0)
        k *= 2
    return x


def _ssd_chunk(rev, cc, xbc_ref, dt_ref, z_ref, alog_ref, dskip_ref, nw_ref, o_ref, state_ref, yacc_ref, *,
               heads, inner):
    q = CHUNK
    hpg = heads // SSD_GROUPS
    gw = hpg * SSD_HEAD_DIM
    n = SSD_STATE
    d = 1 if rev else 0

    dtf = dt_ref[...]
    da = dtf * (-jnp.exp(alog_ref[...]))
    cs = _cumsum_rows(da)
    tot = cs[q - 1:q, :]
    acs = (tot - cs + da) if rev else cs
    acs_t = acs.T
    dt_t = dtf.T
    tot_t = acs_t[:, 0:1] if rev else acs_t[:, q - 1:q]
    w_state_t = jnp.exp(tot_t - acs_t) * dt_t
    w_off = jnp.exp(acs)
    e_tot = jnp.exp(tot)

    li = lax.broadcasted_iota(jnp.int32, (q, q), 0)
    si = lax.broadcasted_iota(jnp.int32, (q, q), 1)
    valid = (li <= si) if rev else (li >= si)
    lane = lax.broadcasted_iota(jnp.int32, (q, LANES), 1)
    lo = lane < SSD_HEAD_DIM

    for g in range(SSD_GROUPS):
        bg = xbc_ref[:, inner + g * n:inner + (g + 1) * n]
        cg = xbc_ref[:, inner + (SSD_GROUPS + g) * n:inner + (SSD_GROUPS + g + 1) * n]
        cb = lax.dot_general(cg, bg, (((1,), (1,)), ((), ())), preferred_element_type=F32)
        bg_t = bg.astype(F32).T
        s_old = state_ref[g]
        y_off = jnp.dot(cg, s_old.astype(BF16), preferred_element_type=F32)
        for pr in range(hpg // 2):
            h1 = g * hpg + 2 * pr
            l1 = d * heads + h1
            l2 = l1 + 1
            cols = slice(h1 * SSD_HEAD_DIM, h1 * SSD_HEAD_DIM + LANES)
            pcols = slice(pr * LANES, (pr + 1) * LANES)

            def decay_mat(l):
                seg = acs[:, l:l + 1] - acs_t[l:l + 1, :]
                return cb * jnp.exp(jnp.where(valid, seg, NEG_INF)) * dt_t[l:l + 1, :]

            lhs = jnp.concatenate([decay_mat(l1), decay_mat(l2)], axis=1).astype(BF16)
            xp = xbc_ref[:, cols].astype(F32)
            rhs = jnp.concatenate([jnp.where(lo, xp, 0.0), jnp.where(lo, 0.0, xp)],
                                  axis=0).astype(BF16)
            y_diag = jnp.dot(lhs, rhs, preferred_element_type=F32)
            w_pair = jnp.where(lo, w_off[:, l1:l1 + 1], w_off[:, l2:l2 + 1])
            y_pair = y_diag + y_off[:, pcols] * w_pair
            lhs_s = jnp.concatenate([bg_t * w_state_t[l1:l1 + 1, :], bg_t * w_state_t[l2:l2 + 1, :]],
                                    axis=1).astype(BF16)
            s_new = jnp.dot(lhs_s, rhs, preferred_element_type=F32)
            dec = jnp.where(lo[0:1, :], e_tot[:, l1:l1 + 1], e_tot[:, l2:l2 + 1])
            state_ref[g, :, pcols] = s_old[:, pcols] * dec + s_new
            if rev:
                yacc_ref[cc, :, cols] = y_pair
            else:
                yacc_ref[cc, :, cols] = yacc_ref[cc, :, cols] + y_pair

    if not rev:
        y = yacc_ref[cc] + xbc_ref[:, 0:inner].astype(F32) * dskip_ref[...]
        y = y * _silu(z_ref[...].astype(F32))
        for g in range(SSD_GROUPS):
            yg = y[:, g * gw:(g + 1) * gw]
            o_ref[:, g * gw:(g + 1) * gw] = _rms(yg, nw_ref[:, g * gw:(g + 1) * gw]).astype(o_ref.dtype)


def _ssd_kernel(xbc_ref, dt_ref, z_ref, alog_ref, dskip_ref, nw_ref, o_ref, state_ref, yacc_ref, *, nct, nlt,
                heads, inner):
    ph = pl.program_id(1)
    c = pl.program_id(2)

    @pl.when(c == 0)
    def _():
        state_ref[...] = jnp.zeros_like(state_ref)

    rev_cc = jnp.where(c < nct, nct - 1 - c, 2 * nct + nlt - 1 - c)
    for rev in (True, False):

        @pl.when(ph == (0 if rev else 1))
        def _(rev=rev):
            _ssd_chunk(rev, rev_cc if rev else c, xbc_ref, dt_ref, z_ref, alog_ref, dskip_ref, nw_ref, o_ref,
                       state_ref, yacc_ref, heads=heads, inner=inner)


def _sgu_kernel(u_ref, v_ref, w_ref, bias_ref, o_ref, *, chunks, groups):
    for ch in range(chunks):
        rows = slice(ch * CHUNK, (ch + 1) * CHUNK)
        gv = _gelu_tanh(v_ref[rows, :].astype(F32))
        mu = jnp.mean(gv, axis=-1, keepdims=True)
        dv = gv - mu
        vn = (dv * lax.rsqrt(jnp.mean(dv * dv, axis=-1, keepdims=True) + NORM_EPS)).astype(BF16)
        gu = _gelu_tanh(u_ref[rows, :].astype(F32))
        for g in range(groups):
            cols = slice(g * LANES, (g + 1) * LANES)
            mixed = jnp.dot(w_ref[g], vn[:, cols], preferred_element_type=F32) + bias_ref[:, cols]
            o_ref[rows, cols] = (gu[:, cols] * mixed).astype(o_ref.dtype)


def _attn_kernel(sink_ref, q_ref, kp_ref, kc_ref, kn_ref, vp_ref, vc_ref, vn_ref, kx_ref, vx_ref, bias_ref,
                 o_ref, *, kv_heads):
    hd = ATT_HEAD_DIM
    qb = CHUNK
    bias = jnp.concatenate([bias_ref[...]] * ATT_GROUP, axis=0)
    row = lax.broadcasted_iota(jnp.int32, (ATT_GROUP * qb, 1), 0)
    for kk in range(kv_heads):
        kcols = slice(kk * hd, (kk + 1) * hd)
        k_all = jnp.concatenate([kp_ref[:, kcols], kc_ref[:, kcols], kn_ref[:, kcols], kx_ref[:, kcols]], axis=0)
        v_all = jnp.concatenate([vp_ref[:, kcols], vc_ref[:, kcols], vn_ref[:, kcols], vx_ref[:, kcols]], axis=0)
        q_all = jnp.concatenate(
            [q_ref[:, (kk * ATT_GROUP + g) * hd:(kk * ATT_GROUP + g + 1) * hd] for g in range(ATT_GROUP)], axis=0)
        s = lax.dot_general(q_all, k_all, (((1,), (1,)), ((), ())), preferred_element_type=F32) + bias
        sink = jnp.full((ATT_GROUP * qb, 1), sink_ref[kk * ATT_GROUP], F32)
        for g in range(1, ATT_GROUP):
            sink = jnp.where(row >= g * qb, sink_ref[kk * ATT_GROUP + g], sink)
        m = jnp.maximum(jnp.max(s, axis=-1, keepdims=True), sink)
        p = jnp.exp(s - m)
        denom = jnp.sum(p, axis=-1, keepdims=True) + jnp.exp(sink - m)
        o = jnp.dot(p.astype(BF16), v_all, preferred_element_type=F32) / denom
        for g in range(ATT_GROUP):
            h = kk * ATT_GROUP + g
            o_ref[:, h * hd:(h + 1) * hd] = o[g * qb:(g + 1) * qb, :].astype(o_ref.dtype)


def _rope_tables(seq, tm):
    pos = jnp.arange(seq, dtype=jnp.int32)
    row = (pos // GRID_W).astype(F32)
    col = (pos % GRID_W).astype(F32)
    quarter = ATT_HEAD_DIM // 4
    inv = ROPE_BASE ** (-jnp.arange(quarter, dtype=F32) / quarter)
    ang = jnp.stack([row[:, None] * inv, col[:, None] * inv], axis=1)
    cos = jnp.repeat(jnp.cos(ang), 2, axis=1).reshape(seq, ATT_HEAD_DIM)
    sin = jnp.sin(ang)
    sin = jnp.stack([-sin, sin], axis=2).reshape(seq, ATT_HEAD_DIM)
    cos = jnp.concatenate([jnp.ones((tm, ATT_HEAD_DIM), F32), cos], axis=0)
    sin = jnp.concatenate([jnp.zeros((tm, ATT_HEAD_DIM), F32), sin], axis=0)
    return cos, sin


def _attn_bias_table(ctx_len):
    i = jnp.arange(CHUNK)[:, None]
    j = jnp.arange(CHUNK)[None, :]
    left = j >= i
    right = j <= i
    on = jnp.ones((CHUNK, CHUNK), bool)
    off = jnp.zeros((CHUNK, CHUNK), bool)
    ctx = jnp.ones((CHUNK, ctx_len), bool)
    kinds = [
        [off, off, off],
        [left, on, right],
        [left, on, off],
        [off, on, right],
        [off, on, off],
    ]
    masks = jnp.stack([jnp.concatenate(k + [ctx], axis=1) for k in kinds])
    return jnp.where(masks, 0.0, NEG_INF).astype(F32)


def kernel(x, c, ctx, c_ctx, w_mod, b_mod, norm_w, w_ffn_in, w_ffn_out, e_w_in, e_conv_w, e_conv_b, e_dt_bias,
           e_a_log, e_d_skip, e_ssd_norm_w, e_sgu_w, e_sgu_b, e_w_out, o_w_qkv, o_sink, o_w_out):
    batch, seq, d = x.shape
    ctx_len = ctx.shape[1]
    depth = w_mod.shape[0]
    ffn_hidden = w_ffn_out.shape[1]
    inner = d // 2
    heads = inner // SSD_HEAD_DIM
    hpg = heads // SSD_GROUPS
    xbc_dim = inner + 2 * SSD_GROUPS * SSD_STATE
    sgu_width = d - inner
    sgu_groups = sgu_width // LANES
    taps = e_conv_w.shape[1]
    att_heads = d // ATT_HEAD_DIM
    kv_heads = att_heads // ATT_GROUP
    q_cols = att_heads * ATT_HEAD_DIM
    kv_cols = kv_heads * ATT_HEAD_DIM

    nc = batch * ctx_len
    nl = batch * seq
    m = nc + nl
    nct = ctx_len // CHUNK
    nlt = seq // CHUNK
    assert ctx_len % CHUNK == 0 and seq % CHUNK == 0 and seq % GRID_W == 0
    assert hpg % 2 == 0 and 2 * heads <= DT_PAD and (3 * inner) % xbc_dim == 0 and inner == sgu_width
    assert q_cols == d and taps // 2 <= HALO

    tm = _pick(math.gcd(nc, seq), 1024)
    tr = _pick(math.gcd(nc, seq), 512)
    n_mod = 1 + batch
    mod_rows = 8

    def mod_row(i, t):
        return jnp.where(i < nc // t, 0, 1 + (i - nc // t) // (seq // t))

    cond = jnp.concatenate([c_ctx[None, :], c, jnp.zeros((mod_rows - n_mod, d), F32)], axis=0)
    mod = _modulation(cond, w_mod, b_mod).reshape(depth * mod_rows, 1, 6 * d)

    def mod_spec(layer, part, t):
        return pl.BlockSpec((None, 1, d), lambda i, *_: (layer * mod_rows + mod_row(i, t), 0, part))

    def vec_spec(row):
        return pl.BlockSpec((None, 1, d), lambda i, *_: (row, 0, 0))

    norm_w3 = norm_w.reshape(depth * 4, 1, d)

    xs = jnp.concatenate([ctx.reshape(nc, d), x.reshape(nl, d)], axis=0)
    h = pl.pallas_call(
        _norm_mod_kernel,
        grid=(m // tr,),
        in_specs=[pl.BlockSpec((tr, d), lambda i: (i, 0)), vec_spec(0), mod_spec(0, 1, tr), mod_spec(0, 0, tr)],
        out_specs=pl.BlockSpec((tr, d), lambda i: (i, 0)),
        out_shape=jax.ShapeDtypeStruct((m, d), BF16),
        compiler_params=_params(("parallel",)),
    )(xs, norm_w3, mod, mod)

    def project(a, w, kernel_fn, out_dtype, tn, extra_in=(), extra_specs=(), w_specs=None, n_out=None):
        n_out = w.shape[1] if n_out is None else n_out
        if w_specs is None:
            w_specs = [pl.BlockSpec((d, tn), lambda i, j: (0, j))]
            ws = (w,)
        else:
            ws = (w,) * len(w_specs)
        return pl.pallas_call(
            kernel_fn,
            grid=(m // tm, n_out // tn),
            in_specs=[pl.BlockSpec((tm, d), lambda i, j: (i, 0))] + w_specs + list(extra_specs),
            out_specs=pl.BlockSpec((tm, tn), lambda i, j: (i, j)),
            out_shape=jax.ShapeDtypeStruct((m, n_out), out_dtype),
            compiler_params=_params(("parallel", "arbitrary")),
        )(a, *ws, *extra_in)

    def out_project(a_list, w, xs_in, layer, sub, next_layer):
        kdim = w.shape[0]
        n_a = len(a_list)
        tk = a_list[0].shape[1] if n_a > 1 else _pick(kdim, 1024 if kdim <= 2048 else 512)
        nk = kdim // tk
        has_next = next_layer is not None
        if n_a > 1:
            a_specs = [pl.BlockSpec((tr, tk), lambda i, k: (i, 0)) for _ in a_list]
        else:
            a_specs = [pl.BlockSpec((tr, tk), lambda i, k: (i, k))]
        gate_part = 2 if sub == 0 else 5
        in_specs = a_specs + [
            pl.BlockSpec((tk, d), lambda i, k: (k, 0)),
            pl.BlockSpec((tr, d), lambda i, k: (i, 0)),
            mod_spec(layer, gate_part, tr),
            vec_spec(layer * 4 + (1 if sub == 0 else 3)),
        ]
        args = list(a_list) + [w, xs_in, mod, norm_w3]
        out_specs = [pl.BlockSpec((tr, d), lambda i, k: (i, 0))]
        out_shape = [jax.ShapeDtypeStruct((m, d), F32)]
        if has_next:
            nl_, nsub = next_layer
            in_specs += [vec_spec(nl_ * 4 + (0 if nsub == 0 else 2)),
                         mod_spec(nl_, 1 if nsub == 0 else 4, tr), mod_spec(nl_, 0 if nsub == 0 else 3, tr)]
            args += [norm_w3, mod, mod]
            out_specs.append(pl.BlockSpec((tr, d), lambda i, k: (i, 0)))
            out_shape.append(jax.ShapeDtypeStruct((m, d), BF16))
        res = pl.pallas_call(
            functools.partial(_mm_resid_kernel, n_a=n_a, nk=nk, has_next=has_next),
            grid=(m // tr, nk),
            in_specs=in_specs,
            out_specs=out_specs,
            out_shape=out_shape,
            scratch_shapes=[pltpu.VMEM((tr, d), F32)],
            compiler_params=_params(("parallel", "arbitrary")),
        )(*args)
        return (res[0], res[1]) if has_next else (res[0], None)

    cos_t, sin_t = _rope_tables(seq, tm)
    bias_t = _attn_bias_table(ctx_len)
    nctb = nc // CHUNK

    for layer in range(depth):
        i2 = layer // 2
        if layer % 2 == 0:
            w_in = e_w_in[i2]
            z_end, xbc_end, dt_end, u_end = inner, inner + xbc_dim, inner + xbc_dim + 2 * heads, \
                inner + xbc_dim + 2 * heads + sgu_width
            w_main = jnp.concatenate([w_in[:, :z_end], w_in[:, dt_end:], w_in[:, z_end:xbc_end]], axis=1).astype(BF16)
            w_dt = jnp.pad(w_in[:, xbc_end:dt_end], ((0, 0), (0, DT_PAD - 2 * heads))).astype(BF16)
            dt_bias = jnp.pad(e_dt_bias[i2].reshape(1, 2 * heads), ((0, 0), (0, DT_PAD - 2 * heads)))
            n_main = w_main.shape[1]
            p = project(h, w_main, _mm_kernel, BF16, _pick(math.gcd(inner, xbc_dim), 512))
            dt = project(h, w_dt, _mm_softplus_kernel, F32, DT_PAD, extra_in=(dt_bias,),
                         extra_specs=[pl.BlockSpec((1, DT_PAD), lambda i, j: (0, 0))])

            cr = _pick(math.gcd(ctx_len, seq), 256)
            xcol = (3 * inner) // xbc_dim
            hb = cr // HALO
            last_hb = m // HALO - 1
            xbc = pl.pallas_call(
                functools.partial(_conv_kernel, rows=cr, taps=taps, ctx_blocks=nc // cr,
                                  ctx_per_seq=ctx_len // cr, lat_per_seq=seq // cr),
                grid=(m // cr,),
                in_specs=[
                    pl.BlockSpec((HALO, xbc_dim), lambda i: (jnp.maximum(i * hb - 1, 0), xcol)),
                    pl.BlockSpec((cr, xbc_dim), lambda i: (i, xcol)),
                    pl.BlockSpec((HALO, xbc_dim), lambda i: (jnp.minimum((i + 1) * hb, last_hb), xcol)),
                    pl.BlockSpec((taps, xbc_dim), lambda i: (0, 0)),
                    pl.BlockSpec((1, xbc_dim), lambda i: (0, 0)),
                ],
                out_specs=pl.BlockSpec((cr, xbc_dim), lambda i: (i, 0)),
                out_shape=jax.ShapeDtypeStruct((m, xbc_dim), BF16),
                scratch_shapes=[pltpu.VMEM((cr + 2 * HALO, xbc_dim), F32)],
                compiler_params=_params(("parallel",)),
            )(p, p, p, e_conv_w[i2], e_conv_b[i2].reshape(1, xbc_dim))

            def chunk_pos(ph, cidx):
                rev_cc = jnp.where(cidx < nct, nct - 1 - cidx, 2 * nct + nlt - 1 - cidx)
                return jnp.where(ph == 0, rev_cc, cidx)

            def row_block(b, cc):
                return jnp.where(cc < nct, b * nct + cc, batch * nct + b * nlt + cc - nct)

            def in_map(b, ph, cidx):
                return (row_block(b, chunk_pos(ph, cidx)), 0)

            def fwd_only_map(b, ph, cidx):
                return (row_block(b, jnp.where(ph == 0, 0, cidx)), 0)

            a_log = jnp.pad(e_a_log[i2].reshape(1, 2 * heads), ((0, 0), (0, DT_PAD - 2 * heads)))
            d_skip = jnp.repeat(e_d_skip[i2], SSD_HEAD_DIM).reshape(1, inner)
            y_ssd = pl.pallas_call(
                functools.partial(_ssd_kernel, nct=nct, nlt=nlt, heads=heads, inner=inner),
                grid=(batch, 2, nct + nlt),
                in_specs=[
                    pl.BlockSpec((CHUNK, xbc_dim), in_map),
                    pl.BlockSpec((CHUNK, DT_PAD), in_map),
                    pl.BlockSpec((CHUNK, inner), fwd_only_map),
                    pl.BlockSpec((1, DT_PAD), lambda b, ph, cidx: (0, 0)),
                    pl.BlockSpec((1, inner), lambda b, ph, cidx: (0, 0)),
                    pl.BlockSpec((1, inner), lambda b, ph, cidx: (0, 0)),
                ],
                out_specs=pl.BlockSpec((CHUNK, inner), fwd_only_map),
                out_shape=jax.ShapeDtypeStruct((m, inner), BF16),
                scratch_shapes=[pltpu.VMEM((SSD_GROUPS, SSD_STATE, hpg * SSD_HEAD_DIM), F32),
                                pltpu.VMEM((nct + nlt, CHUNK, inner), F32)],
                compiler_params=_params(("arbitrary", "arbitrary", "arbitrary")),
            )(xbc, dt, p, a_log, d_skip, e_ssd_norm_w[i2].reshape(1, inner))

            sr = _pick(math.gcd(ctx_len, seq), 256)
            sgu_bias = jnp.repeat(e_sgu_b[i2].T, LANES, axis=1)
            y_sgu = pl.pallas_call(
                functools.partial(_sgu_kernel, chunks=sr // CHUNK, groups=sgu_groups),
                grid=(m // sr,),
                in_specs=[
                    pl.BlockSpec((sr, sgu_width), lambda i: (i, 1)),
                    pl.BlockSpec((sr, sgu_width), lambda i: (i, 2)),
                    pl.BlockSpec((sgu_groups, CHUNK, CHUNK), lambda i: (0, 0, 0)),
                    pl.BlockSpec((CHUNK, sgu_width), lambda i: (0, 0)),
                ],
                out_specs=pl.BlockSpec((sr, sgu_width), lambda i: (i, 0)),
                out_shape=jax.ShapeDtypeStruct((m, sgu_width), BF16),
                compiler_params=_params(("parallel",)),
            )(p, p, e_sgu_w[i2].astype(BF16), sgu_bias)

            xs, h = out_project([y_ssd, y_sgu], e_w_out[i2].astype(BF16), xs, layer, 0, (layer, 1))
        else:
            tn = _pick(kv_cols, 512)
            nblk = seq // tm
            rope_specs = [pl.BlockSpec((tm, ATT_HEAD_DIM),
                                       lambda i, j: (jnp.where(i < nc // tm, 0, 1 + (i - nc // tm) % nblk), 0))] * 2
            p = project(h, o_w_qkv[i2].astype(BF16),
                        functools.partial(_mm_rope_kernel, q_blocks=q_cols // tn, rope_blocks=(q_cols + kv_cols) // tn,
                                          scale=ATT_HEAD_DIM ** -0.5),
                        BF16, tn, extra_in=(cos_t, sin_t), extra_specs=rope_specs)

            def qblock(i):
                is_ctx = i < nctb
                return is_ctx, jnp.where(is_ctx, i // nct, (i - nctb) // nlt), jnp.where(is_ctx, 0, (i - nctb) % nlt)

            def win_map(off, colblk):
                def f(i):
                    is_ctx, b, nblk_ = qblock(i)
                    nb = jnp.clip(nblk_ + off, 0, nlt - 1)
                    return (nctb + b * nlt + nb, colblk)
                return f

            def ctx_map(colblk):
                def f(i):
                    return (qblock(i)[1], colblk)
                return f

            def kind_map(i):
                is_ctx, b, nblk_ = qblock(i)
                kind = 1 + 2 * (nblk_ == 0).astype(jnp.int32) + (nblk_ == nlt - 1).astype(jnp.int32)
                return (jnp.where(is_ctx, 0, kind), 0, 0)

            kblk = q_cols // kv_cols
            att = pl.pallas_call(
                functools.partial(_attn_kernel, kv_heads=kv_heads),
                grid=(m // CHUNK,),
                in_specs=[
                    pl.BlockSpec(memory_space=pltpu.SMEM),
                    pl.BlockSpec((CHUNK, q_cols), lambda i: (i, 0)),
                    pl.BlockSpec((CHUNK, kv_cols), win_map(-1, kblk)),
                    pl.BlockSpec((CHUNK, kv_cols), win_map(0, kblk)),
                    pl.BlockSpec((CHUNK, kv_cols), win_map(1, kblk)),
                    pl.BlockSpec((CHUNK, kv_cols), win_map(-1, kblk + 1)),
                    pl.BlockSpec((CHUNK, kv_cols), win_map(0, kblk + 1)),
                    pl.BlockSpec((CHUNK, kv_cols), win_map(1, kblk + 1)),
                    pl.BlockSpec((ctx_len, kv_cols), ctx_map(kblk)),
                    pl.BlockSpec((ctx_len, kv_cols), ctx_map(kblk + 1)),
                    pl.BlockSpec((None, CHUNK, 3 * CHUNK + ctx_len), kind_map),
                ],
                out_specs=pl.BlockSpec((CHUNK, q_cols), lambda i: (i, 0)),
                out_shape=jax.ShapeDtypeStruct((m, q_cols), BF16),
                compiler_params=_params(("parallel",)),
            )(o_sink[i2], p, p, p, p, p, p, p, p, p, bias_t)

            xs, h = out_project([att], o_w_out[i2].astype(BF16), xs, layer, 0, (layer, 1))

        tf = _pick(ffn_hidden, 512)
        nfb = ffn_hidden // tf
        act = project(h, w_ffn_in[layer].astype(BF16), _mm_swiglu_kernel, BF16, tf,
                      w_specs=[pl.BlockSpec((d, tf), lambda i, j: (0, j)),
                               pl.BlockSpec((d, tf), lambda i, j: (0, j + nfb))],
                      n_out=ffn_hidden)
        xs, h = out_project([act], w_ffn_out[layer].astype(BF16), xs, layer, 1,
                            (layer + 1, 0) if layer + 1 < depth else None)

    return xs[nc:].reshape(batch, seq, d)
```

```python
import functools
import math

import jax
import jax.numpy as jnp
from jax import lax
from jax.experimental import pallas as pl
from jax.experimental.pallas import tpu as pltpu

F32 = jnp.float32
BF16 = jnp.bfloat16

NORM_EPS = 1e-6
NEG_INF = -1e30
GRID_W = 64
ROPE_BASE = 10000.0
SSD_HEAD_DIM = 64
SSD_GROUPS = 2
SSD_STATE = 128
ATT_HEAD_DIM = 128
ATT_GROUP = 4
CHUNK = 128
LANES = 128
DT_PAD = LANES
HALO = 16
VMEM_LIMIT_BYTES = 56 * 1024 * 1024


def _pick(n, pref):
    best = None
    for t in range(LANES, min(n, pref) + 1, LANES):
        if n % t == 0:
            best = t
    assert best is not None, (n, pref)
    return best


def _params(sem):
    return pltpu.CompilerParams(dimension_semantics=sem, vmem_limit_bytes=VMEM_LIMIT_BYTES)


def _rms(x, w):
    return x * lax.rsqrt(jnp.mean(x * x, axis=-1, keepdims=True) + NORM_EPS) * w


def _silu(x):
    return x * jax.nn.sigmoid(x)


def _gelu_tanh(x):
    return 0.5 * x * (1.0 + jnp.tanh(math.sqrt(2.0 / math.pi) * (x + 0.044715 * (x * x * x))))


def _mod_kernel(c_ref, w_ref, b_ref, o_ref):
    s = _silu(c_ref[...]).astype(BF16)
    o_ref[0] = jnp.dot(s, w_ref[0].astype(BF16), preferred_element_type=F32) + b_ref[0]


def _modulation(cond, w_mod, b_mod):
    depth, d, n = w_mod.shape
    rows = cond.shape[0]
    tn = _pick(n, 1024)
    return pl.pallas_call(
        _mod_kernel,
        grid=(depth, n // tn),
        in_specs=[
            pl.BlockSpec((rows, d), lambda l, j: (0, 0)),
            pl.BlockSpec((1, d, tn), lambda l, j: (l, 0, j)),
            pl.BlockSpec((1, 1, tn), lambda l, j: (l, 0, j)),
        ],
        out_specs=pl.BlockSpec((1, rows, tn), lambda l, j: (l, 0, j)),
        out_shape=jax.ShapeDtypeStruct((depth, rows, n), F32),
        compiler_params=_params(("arbitrary", "arbitrary")),
    )(cond, w_mod, b_mod.reshape(depth, 1, n))


def _stream_norm_kernel(c_ref, x_ref, nw_ref, sc_ref, sh_ref, xs_ref, h_ref, *, ctx_blocks):
    xv = jnp.where(pl.program_id(0) < ctx_blocks, c_ref[...], x_ref[...])
    xs_ref[...] = xv
    h_ref[...] = (_rms(xv, nw_ref[...]) * (1.0 + sc_ref[...]) + sh_ref[...]).astype(BF16)


def _mm_kernel(a_ref, w_ref, o_ref):
    o_ref[...] = jnp.dot(a_ref[...], w_ref[...], preferred_element_type=F32).astype(o_ref.dtype)


def _mm_softplus_kernel(a_ref, w_ref, b_ref, o_ref):
    p = jnp.dot(a_ref[...], w_ref[...], preferred_element_type=F32) + b_ref[...]
    o_ref[...] = jnp.maximum(p, 0.0) + jnp.log1p(jnp.exp(-jnp.abs(p)))


def _mm_rope_kernel(a_ref, w_ref, cos_ref, sin_ref, o_ref, *, q_blocks, rope_blocks, scale):
    j = pl.program_id(1)
    acc = jnp.dot(a_ref[...], w_ref[...], preferred_element_type=F32)
    s = jnp.where(j < q_blocks, scale, 1.0).astype(F32)
    rot = j < rope_blocks
    cos = jnp.where(rot, cos_ref[...], 1.0) * s
    sin = jnp.where(rot, sin_ref[...], 0.0) * s
    for c in range(acc.shape[1] // ATT_HEAD_DIM):
        xc = acc[:, c * ATT_HEAD_DIM:(c + 1) * ATT_HEAD_DIM]
        o_ref[:, c * ATT_HEAD_DIM:(c + 1) * ATT_HEAD_DIM] = (
            xc * cos + pltpu.roll(xc, ATT_HEAD_DIM // 2, 1) * sin).astype(o_ref.dtype)


def _resid_epilogue(o, x_ref, g_ref, nw_ref, nxt, xo_ref, ho_ref):
    xn = x_ref[...] + g_ref[...] * _rms(o, nw_ref[...])
    xo_ref[...] = xn
    if ho_ref is not None:
        nwn_ref, sc_ref, sh_ref = nxt
        ho_ref[...] = (_rms(xn, nwn_ref[...]) * (1.0 + sc_ref[...]) + sh_ref[...]).astype(BF16)


def _fused_resid_kernel(*refs, mlp, n_a, nblk, ne, se, has_next):
    if mlp:
        h_ref, wg_ref, wu_ref, wo_ref = refs[:4]
        pos = 4
    else:
        a_refs = refs[:n_a]
        w_ref = refs[n_a]
        pos = n_a + 1
    x_ref, g_ref, nw_ref = refs[pos:pos + 3]
    pos += 3
    nxt = refs[pos:pos + 3] if has_next else None
    pos += 3 if has_next else 0
    xo_ref = refs[pos]
    ho_ref = refs[pos + 1] if has_next else None
    accs = refs[-2:]
    nslab, tn = accs[0].shape[0], accs[0].shape[2]
    i = pl.program_id(0)
    j = pl.program_id(1)

    def produce(acc_ref):
        if mlp:
            a = h_ref[...]
            g = jnp.dot(a, wg_ref[...], preferred_element_type=F32)
            u = jnp.dot(a, wu_ref[...], preferred_element_type=F32)
            act = (_silu(g) * u).astype(BF16)
            for n in range(nslab):
                part = jnp.dot(act, wo_ref[:, n * tn:(n + 1) * tn], preferred_element_type=F32)
                acc_ref[n] = jnp.where(j == 0, 0.0, acc_ref[n]) + part
        else:
            val, k0 = None, 0
            for a_ref in a_refs:
                ka = a_ref.shape[1]
                part = jnp.dot(a_ref[...], w_ref[k0:k0 + ka, :], preferred_element_type=F32)
                val = part if val is None else val + part
                k0 += ka
            acc_ref[j] = val

    def epilogue(acc_ref):
        r0 = pl.multiple_of(jnp.minimum(j, ne - 1) * se, se)
        o = jnp.concatenate([acc_ref[n, pl.ds(r0, se), :] for n in range(nslab)], axis=1)
        _resid_epilogue(o, x_ref, g_ref, nw_ref, nxt, xo_ref, ho_ref)

    @pl.when(jnp.logical_and(i == 0, j == 0))
    def _():
        for acc_ref in accs:
            acc_ref[...] = jnp.zeros_like(acc_ref)

    @pl.when(i == 0)
    def _():
        produce(accs[0])

    for parity in (0, 1):

        @pl.when(jnp.logical_and(jnp.logical_and(i > 0, i < nblk), i % 2 == parity))
        def _(parity=parity):
            epilogue(accs[1 - parity])
            produce(accs[parity])

    @pl.when(jnp.logical_and(i == nblk, j < ne))
    def _():
        epilogue(accs[(nblk - 1) % 2])


def _conv_kernel(prev_ref, cur_ref, next_ref, w_ref, b_ref, o_ref, ext_ref, *, rows, taps, ctx_blocks,
                 ctx_per_seq, lat_per_seq):
    i = pl.program_id(0)
    in_ctx = i < ctx_blocks
    local = jnp.where(in_ctx, i % ctx_per_seq, (i - ctx_blocks) % lat_per_seq)
    per_seq = jnp.where(in_ctx, ctx_per_seq, lat_per_seq)
    first = local == 0
    last = local == per_seq - 1
    ext_ref[0:HALO, :] = jnp.where(first, 0.0, prev_ref[...].astype(F32))
    ext_ref[HALO:HALO + rows, :] = cur_ref[...].astype(F32)
    ext_ref[HALO + rows:HALO + rows + HALO, :] = jnp.where(last, 0.0, next_ref[...].astype(F32))
    acc = jnp.zeros(o_ref.shape, F32) + b_ref[...]
    for k in range(taps):
        acc = acc + w_ref[k:k + 1, :] * ext_ref[pl.ds(HALO - taps // 2 + k, rows), :]
    o_ref[...] = _silu(acc).astype(o_ref.dtype)


def _cumsum_rows(x):
    rows = x.shape[0]
    row = lax.broadcasted_iota(jnp.int32, x.shape, 0)
    k = 1
    while k < rows:
        x = x + jnp.where(row >= k, pltpu.roll(x, k, 0), 0.0)
        k *= 2
    return x


def _ssd_chunk(rev, cc, xbc_ref, dt_ref, z_ref, alog_ref, dskip_ref, nw_ref, o_ref, state_ref, yacc_ref, *,
               heads, inner):
    q = CHUNK
    hpg = heads // SSD_GROUPS
    gw = hpg * SSD_HEAD_DIM
    n = SSD_STATE
    d = 1 if rev else 0

    dtf = dt_ref[...]
    da = dtf * (-jnp.exp(alog_ref[...]))
    cs = _cumsum_rows(da)
    tot = cs[q - 1:q, :]
    acs = (tot - cs + da) if rev else cs
    acs_t = acs.T
    dt_t = dtf.T
    tot_t = acs_t[:, 0:1] if rev else acs_t[:, q - 1:q]
    w_state_t = jnp.exp(tot_t - acs_t) * dt_t
    w_off = jnp.exp(acs)
    e_tot = jnp.exp(tot)

    li = lax.broadcasted_iota(jnp.int32, (q, q), 0)
    si = lax.broadcasted_iota(jnp.int32, (q, q), 1)
    valid = (li <= si) if rev else (li >= si)
    lane = lax.broadcasted_iota(jnp.int32, (q, LANES), 1)
    lo = lane < SSD_HEAD_DIM

    for g in range(SSD_GROUPS):
        bg = xbc_ref[:, inner + g * n:inner + (g + 1) * n]
        cg = xbc_ref[:, inner + (SSD_GROUPS + g) * n:inner + (SSD_GROUPS + g + 1) * n]
        cb = lax.dot_general(cg, bg, (((1,), (1,)), ((), ())), preferred_element_type=F32)
        bg_t = bg.astype(F32).T
        s_old = state_ref[g]
        y_off = jnp.dot(cg, s_old.astype(BF16), preferred_element_type=F32)
        for pr in range(hpg // 2):
            h1 = g * hpg + 2 * pr
            l1 = d * heads + h1
            l2 = l1 + 1
            cols = slice(h1 * SSD_HEAD_DIM, h1 * SSD_HEAD_DIM + LANES)
            pcols = slice(pr * LANES, (pr + 1) * LANES)

            def decay_mat(l):
                seg = acs[:, l:l + 1] - acs_t[l:l + 1, :]
                return cb * jnp.exp(jnp.where(valid, seg, NEG_INF)) * dt_t[l:l + 1, :]

            lhs = jnp.concatenate([decay_mat(l1), decay_mat(l2)], axis=1).astype(BF16)
            xp = xbc_ref[:, cols].astype(F32)
            rhs = jnp.concatenate([jnp.where(lo, xp, 0.0), jnp.where(lo, 0.0, xp)],
                                  axis=0).astype(BF16)
            y_diag = jnp.dot(lhs, rhs, preferred_element_type=F32)
            w_pair = jnp.where(lo, w_off[:, l1:l1 + 1], w_off[:, l2:l2 + 1])
            y_pair = y_diag + y_off[:, pcols] * w_pair
            lhs_s = jnp.concatenate([bg_t * w_state_t[l1:l1 + 1, :], bg_t * w_state_t[l2:l2 + 1, :]],
                                    axis=1).astype(BF16)
            s_new = jnp.dot(lhs_s, rhs, preferred_element_type=F32)
            dec = jnp.where(lo[0:1, :], e_tot[:, l1:l1 + 1], e_tot[:, l2:l2 + 1])
            state_ref[g, :, pcols] = s_old[:, pcols] * dec + s_new
            if rev:
                yacc_ref[cc, :, cols] = y_pair
            else:
                yacc_ref[cc, :, cols] = yacc_ref[cc, :, cols] + y_pair

    if not rev:
        y = yacc_ref[cc] + xbc_ref[:, 0:inner].astype(F32) * dskip_ref[...]
        y = y * _silu(z_ref[...].astype(F32))
        for g in range(SSD_GROUPS):
            yg = y[:, g * gw:(g + 1) * gw]
            o_ref[:, g * gw:(g + 1) * gw] = _rms(yg, nw_ref[:, g * gw:(g + 1) * gw]).astype(o_ref.dtype)


def _ssd_kernel(xbc_ref, dt_ref, z_ref, alog_ref, dskip_ref, nw_ref, o_ref, state_ref, yacc_ref, *, nct, nlt,
                heads, inner):
    ph = pl.program_id(1)
    c = pl.program_id(2)

    @pl.when(c == 0)
    def _():
        state_ref[...] = jnp.zeros_like(state_ref)

    rev_cc = jnp.where(c < nct, nct - 1 - c, 2 * nct + nlt - 1 - c)
    for rev in (True, False):

        @pl.when(ph == (0 if rev else 1))
        def _(rev=rev):
            _ssd_chunk(rev, rev_cc if rev else c, xbc_ref, dt_ref, z_ref, alog_ref, dskip_ref, nw_ref, o_ref,
                       state_ref, yacc_ref, heads=heads, inner=inner)


def _sgu_kernel(u_ref, v_ref, w_ref, bias_ref, o_ref, *, chunks, groups):
    for ch in range(chunks):
        rows = slice(ch * CHUNK, (ch + 1) * CHUNK)
        gv = _gelu_tanh(v_ref[rows, :].astype(F32))
        mu = jnp.mean(gv, axis=-1, keepdims=True)
        dv = gv - mu
        vn = (dv * lax.rsqrt(jnp.mean(dv * dv, axis=-1, keepdims=True) + NORM_EPS)).astype(BF16)
        gu = _gelu_tanh(u_ref[rows, :].astype(F32))
        for g in range(groups):
            cols = slice(g * LANES, (g + 1) * LANES)
            mixed = jnp.dot(w_ref[g], vn[:, cols], preferred_element_type=F32) + bias_ref[:, cols]
            o_ref[rows, cols] = (gu[:, cols] * mixed).astype(o_ref.dtype)


def _attn_kernel(sink_ref, q_ref, kp_ref, kc_ref, kn_ref, vp_ref, vc_ref, vn_ref, kx_ref, vx_ref, bias_ref,
                 o_ref, *, kv_heads):
    hd = ATT_HEAD_DIM
    qb = CHUNK
    nwin = 3 * CHUNK
    bias = jnp.concatenate([bias_ref[...]] * ATT_GROUP, axis=0)
    row = lax.broadcasted_iota(jnp.int32, (ATT_GROUP * qb, 1), 0)
    nkeys = nwin + kx_ref.shape[0]
    ones = jnp.ones((nkeys, hd), BF16)
    for kk in range(kv_heads):
        kcols = slice(kk * hd, (kk + 1) * hd)
        k_all = jnp.concatenate([kp_ref[:, kcols], kc_ref[:, kcols], kn_ref[:, kcols], kx_ref[:, kcols]], axis=0)
        v_all = jnp.concatenate([vp_ref[:, kcols], vc_ref[:, kcols], vn_ref[:, kcols], vx_ref[:, kcols]], axis=0)
        q_all = jnp.concatenate(
            [q_ref[:, (kk * ATT_GROUP + g) * hd:(kk * ATT_GROUP + g + 1) * hd] for g in range(ATT_GROUP)], axis=0)
        s = lax.dot_general(q_all, k_all, (((1,), (1,)), ((), ())), preferred_element_type=F32)
        s = jnp.concatenate([s[:, :nwin] + bias, s[:, nwin:]], axis=1)
        sink = jnp.full((ATT_GROUP * qb, 1), sink_ref[kk * ATT_GROUP], F32)
        for g in range(1, ATT_GROUP):
            sink = jnp.where(row >= g * qb, sink_ref[kk * ATT_GROUP + g], sink)
        m = jnp.maximum(jnp.max(s, axis=-1, keepdims=True), sink)
        p = jnp.exp(s - m).astype(BF16)
        ov = jnp.dot(p, jnp.concatenate([v_all, ones], axis=1), preferred_element_type=F32)
        denom = ov[:, hd:hd + 1] + jnp.exp(sink - m)
        o = ov[:, :hd] * (1.0 / denom)
        for g in range(ATT_GROUP):
            h = kk * ATT_GROUP + g
            o_ref[:, h * hd:(h + 1) * hd] = o[g * qb:(g + 1) * qb, :].astype(o_ref.dtype)


def _rope_lane_perm():
    quarter = ATT_HEAD_DIM // 4
    return jnp.arange(ATT_HEAD_DIM).reshape(2, 2, quarter).transpose(1, 0, 2).reshape(-1)


def _rope_tables(seq, tm):
    pos = jnp.arange(seq, dtype=jnp.int32)
    row = (pos // GRID_W).astype(F32)
    col = (pos % GRID_W).astype(F32)
    quarter = ATT_HEAD_DIM // 4
    inv = ROPE_BASE ** (-jnp.arange(quarter, dtype=F32) / quarter)
    ang = jnp.stack([row[:, None] * inv, col[:, None] * inv], axis=1).reshape(seq, 2 * quarter)
    cos = jnp.concatenate([jnp.cos(ang), jnp.cos(ang)], axis=1)
    sin = jnp.concatenate([-jnp.sin(ang), jnp.sin(ang)], axis=1)
    cos = jnp.concatenate([jnp.ones((tm, ATT_HEAD_DIM), F32), cos], axis=0)
    sin = jnp.concatenate([jnp.zeros((tm, ATT_HEAD_DIM), F32), sin], axis=0)
    return cos, sin


def _attn_bias_table():
    i = jnp.arange(CHUNK)[:, None]
    j = jnp.arange(CHUNK)[None, :]
    left = j >= i
    right = j <= i
    on = jnp.ones((CHUNK, CHUNK), bool)
    off = jnp.zeros((CHUNK, CHUNK), bool)
    kinds = [
        [off, off, off],
        [left, on, right],
        [left, on, off],
        [off, on, right],
        [off, on, off],
    ]
    masks = jnp.stack([jnp.concatenate(k, axis=1) for k in kinds])
    return jnp.where(masks, 0.0, NEG_INF).astype(F32)


def kernel(x, c, ctx, c_ctx, w_mod, b_mod, norm_w, w_ffn_in, w_ffn_out, e_w_in, e_conv_w, e_conv_b, e_dt_bias,
           e_a_log, e_d_skip, e_ssd_norm_w, e_sgu_w, e_sgu_b, e_w_out, o_w_qkv, o_sink, o_w_out):
    batch, seq, d = x.shape
    ctx_len = ctx.shape[1]
    depth = w_mod.shape[0]
    ffn_hidden = w_ffn_out.shape[1]
    inner = d // 2
    heads = inner // SSD_HEAD_DIM
    hpg = heads // SSD_GROUPS
    xbc_dim = inner + 2 * SSD_GROUPS * SSD_STATE
    sgu_width = d - inner
    sgu_groups = sgu_width // LANES
    taps = e_conv_w.shape[1]
    att_heads = d // ATT_HEAD_DIM
    kv_heads = att_heads // ATT_GROUP
    q_cols = att_heads * ATT_HEAD_DIM
    kv_cols = kv_heads * ATT_HEAD_DIM

    nc = batch * ctx_len
    nl = batch * seq
    m = nc + nl
    nct = ctx_len // CHUNK
    nlt = seq // CHUNK
    assert ctx_len % CHUNK == 0 and seq % CHUNK == 0 and seq % GRID_W == 0
    assert hpg % 2 == 0 and 2 * heads <= DT_PAD and (3 * inner) % xbc_dim == 0 and inner == sgu_width
    assert q_cols == d and taps // 2 <= HALO

    tm = _pick(math.gcd(nc, seq), 1024)
    tr = _pick(math.gcd(nc, seq), 512)
    ncb = nc // tm
    n_mod = 1 + batch
    mod_rows = 8

    def mod_row(i, t):
        return jnp.where(i < nc // t, 0, 1 + (i - nc // t) // (seq // t))

    cond = jnp.concatenate([c_ctx[None, :], c, jnp.zeros((mod_rows - n_mod, d), F32)], axis=0)
    mod = _modulation(cond, w_mod, b_mod).reshape(depth * mod_rows, 1, 6 * d)
    norm_w3 = norm_w.reshape(depth * 4, 1, d)

    def mod_spec(layer, part, blk):
        return pl.BlockSpec((None, 1, d), lambda *g: (layer * mod_rows + mod_row(blk(*g), tm), 0, part))

    def vec_spec(row):
        return pl.BlockSpec((None, 1, d), lambda *g: (row, 0, 0))

    ctb = nc // tr
    xs, h = pl.pallas_call(
        functools.partial(_stream_norm_kernel, ctx_blocks=ctb),
        grid=(m // tr,),
        in_specs=[
            pl.BlockSpec((tr, d), lambda i: (jnp.minimum(i, ctb - 1), 0)),
            pl.BlockSpec((tr, d), lambda i: (jnp.maximum(i - ctb, 0), 0)),
            vec_spec(0),
            mod_spec(0, 1, lambda i: i * tr // tm),
            mod_spec(0, 0, lambda i: i * tr // tm),
        ],
        out_specs=[pl.BlockSpec((tr, d), lambda i: (i, 0)), pl.BlockSpec((tr, d), lambda i: (i, 0))],
        out_shape=[jax.ShapeDtypeStruct((m, d), F32), jax.ShapeDtypeStruct((m, d), BF16)],
        compiler_params=_params(("parallel",)),
    )(ctx.reshape(nc, d), x.reshape(nl, d), norm_w3, mod, mod)

    def project(a, w, kernel_fn, out_dtype, tn, extra_in=(), extra_specs=()):
        n_out = w.shape[1]
        return pl.pallas_call(
            kernel_fn,
            grid=(m // tm, n_out // tn),
            in_specs=[pl.BlockSpec((tm, d), lambda i, j: (i, 0)), pl.BlockSpec((d, tn), lambda i, j: (0, j))]
            + list(extra_specs),
            out_specs=pl.BlockSpec((tm, tn), lambda i, j: (i, j)),
            out_shape=jax.ShapeDtypeStruct((m, n_out), out_dtype),
            compiler_params=_params(("parallel", "arbitrary")),
        )(a, w, *extra_in)

    def fused_resid(layer, sub, next_layer, xs_in, lat_only, a_list=None, w=None, h_in=None, w_in=None, w_out=None):
        mlp = a_list is None
        rows = nl if lat_only else m
        nblk = rows // tm
        mod_off = ncb if lat_only else 0
        tn = _pick(d, 512)
        nslab = d // tn
        if mlp:
            tf = _pick(ffn_hidden, 512)
            nsteps = ffn_hidden // tf
        else:
            nsteps = nslab
        ne = 1
        while ne * 2 <= min(nsteps, 8) and (tm // (ne * 2)) % 16 == 0:
            ne *= 2
        se = tm // ne
        has_next = next_layer is not None

        def off(arr):
            return (arr.shape[0] - rows) // tm

        def blk_map(arr):
            return lambda i, j: (jnp.minimum(i, nblk - 1) + off(arr), 0)

        def sub_idx(i, j):
            return jnp.where(i == 0, 0, (i - 1) * ne + jnp.minimum(j, ne - 1))

        def prev_blk(i, j):
            return jnp.maximum(i - 1, 0) + mod_off

        if mlp:
            in_specs = [
                pl.BlockSpec((tm, d), blk_map(h_in)),
                pl.BlockSpec((d, tf), lambda i, j: (0, j)),
                pl.BlockSpec((d, tf), lambda i, j: (0, j + nsteps)),
                pl.BlockSpec((tf, d), lambda i, j: (j, 0)),
            ]
            args = [h_in, w_in, w_in, w_out]
        else:
            in_specs = [pl.BlockSpec((tm, a.shape[1]), blk_map(a)) for a in a_list]
            in_specs.append(pl.BlockSpec((w.shape[0], tn), lambda i, j: (0, j)))
            args = list(a_list) + [w]
        x_off = off(xs_in) * ne
        in_specs += [
            pl.BlockSpec((se, d), lambda i, j: (sub_idx(i, j) + x_off, 0)),
            mod_spec(layer, 2 if sub == 0 else 5, prev_blk),
            vec_spec(layer * 4 + (1 if sub == 0 else 3)),
        ]
        args += [xs_in, mod, norm_w3]
        out_specs = [pl.BlockSpec((se, d), lambda i, j: (sub_idx(i, j), 0))]
        out_shape = [jax.ShapeDtypeStruct((rows, d), F32)]
        if has_next:
            nl_, nsub = next_layer
            in_specs += [vec_spec(nl_ * 4 + (0 if nsub == 0 else 2)),
                         mod_spec(nl_, 1 if nsub == 0 else 4, prev_blk),
                         mod_spec(nl_, 0 if nsub == 0 else 3, prev_blk)]
            args += [norm_w3, mod, mod]
            out_specs.append(pl.BlockSpec((se, d), lambda i, j: (sub_idx(i, j), 0)))
            out_shape.append(jax.ShapeDtypeStruct((rows, d), BF16))
        res = pl.pallas_call(
            functools.partial(_fused_resid_kernel, mlp=mlp, n_a=0 if mlp else len(a_list), nblk=nblk, ne=ne, se=se,
                              has_next=has_next),
            grid=(nblk + 1, nsteps),
            in_specs=in_specs,
            out_specs=out_specs,
            out_shape=out_shape,
            scratch_shapes=[pltpu.VMEM((nslab, tm, tn), F32), pltpu.VMEM((nslab, tm, tn), F32)],
            compiler_params=_params(("arbitrary", "arbitrary")),
        )(*args)
        return (res[0], res[1]) if has_next else (res[0], None)

    cos_t, sin_t = _rope_tables(seq, tm)
    bias_t = _attn_bias_table()
    nctb = nc // CHUNK

    for layer in range(depth):
        i2 = layer // 2
        lat_only = layer == depth - 1
        if layer % 2 == 0:
            w_in = e_w_in[i2]
            z_end, xbc_end, dt_end = inner, inner + xbc_dim, inner + xbc_dim + 2 * heads
            w_main = jnp.concatenate([w_in[:, :z_end], w_in[:, dt_end:], w_in[:, z_end:xbc_end]], axis=1).astype(BF16)
            w_dt = jnp.pad(w_in[:, xbc_end:dt_end], ((0, 0), (0, DT_PAD - 2 * heads))).astype(BF16)
            dt_bias = jnp.pad(e_dt_bias[i2].reshape(1, 2 * heads), ((0, 0), (0, DT_PAD - 2 * heads)))
            p = project(h, w_main, _mm_kernel, BF16, _pick(math.gcd(inner, xbc_dim), 512))
            dt = project(h, w_dt, _mm_softplus_kernel, F32, DT_PAD, extra_in=(dt_bias,),
                         extra_specs=[pl.BlockSpec((1, DT_PAD), lambda i, j: (0, 0))])

            cr = _pick(math.gcd(ctx_len, seq), 256)
            xcol = (3 * inner) // xbc_dim
            hb = cr // HALO
            last_hb = m // HALO - 1
            xbc = pl.pallas_call(
                functools.partial(_conv_kernel, rows=cr, taps=taps, ctx_blocks=nc // cr,
                                  ctx_per_seq=ctx_len // cr, lat_per_seq=seq // cr),
                grid=(m // cr,),
                in_specs=[
                    pl.BlockSpec((HALO, xbc_dim), lambda i: (jnp.maximum(i * hb - 1, 0), xcol)),
                    pl.BlockSpec((cr, xbc_dim), lambda i: (i, xcol)),
                    pl.BlockSpec((HALO, xbc_dim), lambda i: (jnp.minimum((i + 1) * hb, last_hb), xcol)),
                    pl.BlockSpec((taps, xbc_dim), lambda i: (0, 0)),
                    pl.BlockSpec((1, xbc_dim), lambda i: (0, 0)),
                ],
                out_specs=pl.BlockSpec((cr, xbc_dim), lambda i: (i, 0)),
                out_shape=jax.ShapeDtypeStruct((m, xbc_dim), BF16),
                scratch_shapes=[pltpu.VMEM((cr + 2 * HALO, xbc_dim), F32)],
                compiler_params=_params(("parallel",)),
            )(p, p, p, e_conv_w[i2], e_conv_b[i2].reshape(1, xbc_dim))

            def chunk_pos(ph, cidx):
                rev_cc = jnp.where(cidx < nct, nct - 1 - cidx, 2 * nct + nlt - 1 - cidx)
                return jnp.where(ph == 0, rev_cc, cidx)

            def row_block(b, cc):
                return jnp.where(cc < nct, b * nct + cc, batch * nct + b * nlt + cc - nct)

            def in_map(b, ph, cidx):
                return (row_block(b, chunk_pos(ph, cidx)), 0)

            def fwd_only_map(b, ph, cidx):
                return (row_block(b, jnp.where(ph == 0, 0, cidx)), 0)

            a_log = jnp.pad(e_a_log[i2].reshape(1, 2 * heads), ((0, 0), (0, DT_PAD - 2 * heads)))
            d_skip = jnp.repeat(e_d_skip[i2], SSD_HEAD_DIM).reshape(1, inner)
            y_ssd = pl.pallas_call(
                functools.partial(_ssd_kernel, nct=nct, nlt=nlt, heads=heads, inner=inner),
                grid=(batch, 2, nct + nlt),
                in_specs=[
                    pl.BlockSpec((CHUNK, xbc_dim), in_map),
                    pl.BlockSpec((CHUNK, DT_PAD), in_map),
                    pl.BlockSpec((CHUNK, inner), fwd_only_map),
                    pl.BlockSpec((1, DT_PAD), lambda b, ph, cidx: (0, 0)),
                    pl.BlockSpec((1, inner), lambda b, ph, cidx: (0, 0)),
                    pl.BlockSpec((1, inner), lambda b, ph, cidx: (0, 0)),
                ],
                out_specs=pl.BlockSpec((CHUNK, inner), fwd_only_map),
                out_shape=jax.ShapeDtypeStruct((m, inner), BF16),
                scratch_shapes=[pltpu.VMEM((SSD_GROUPS, SSD_STATE, hpg * SSD_HEAD_DIM), F32),
                                pltpu.VMEM((nct + nlt, CHUNK, inner), F32)],
                compiler_params=_params(("arbitrary", "arbitrary", "arbitrary")),
            )(xbc, dt, p, a_log, d_skip, e_ssd_norm_w[i2].reshape(1, inner))

            sr = _pick(math.gcd(ctx_len, seq), 256)
            sgu_bias = jnp.repeat(e_sgu_b[i2].T, LANES, axis=1)
            y_sgu = pl.pallas_call(
                functools.partial(_sgu_kernel, chunks=sr // CHUNK, groups=sgu_groups),
                grid=(m // sr,),
                in_specs=[
                    pl.BlockSpec((sr, sgu_width), lambda i: (i, 1)),
                    pl.BlockSpec((sr, sgu_width), lambda i: (i, 2)),
                    pl.BlockSpec((sgu_groups, CHUNK, CHUNK), lambda i: (0, 0, 0)),
                    pl.BlockSpec((CHUNK, sgu_width), lambda i: (0, 0)),
                ],
                out_specs=pl.BlockSpec((sr, sgu_width), lambda i: (i, 0)),
                out_shape=jax.ShapeDtypeStruct((m, sgu_width), BF16),
                compiler_params=_params(("parallel",)),
            )(p, p, e_sgu_w[i2].astype(BF16), sgu_bias)

            xs, h = fused_resid(layer, 0, (layer, 1), xs, lat_only, a_list=[y_ssd, y_sgu],
                                w=e_w_out[i2].astype(BF16))
        else:
            tn = _pick(kv_cols, 512)
            nblk = seq // tm
            perm = _rope_lane_perm()
            n_rot = (q_cols + kv_cols) // ATT_HEAD_DIM
            col_idx = jnp.concatenate([
                (jnp.arange(n_rot)[:, None] * ATT_HEAD_DIM + perm[None, :]).reshape(-1),
                jnp.arange(q_cols + kv_cols, q_cols + 2 * kv_cols)])
            w_qkv = jnp.take(o_w_qkv[i2], col_idx, axis=1).astype(BF16)
            rope_specs = [pl.BlockSpec((tm, ATT_HEAD_DIM),
                                       lambda i, j: (jnp.where(i < ncb, 0, 1 + (i - ncb) % nblk), 0))] * 2
            p = project(h, w_qkv,
                        functools.partial(_mm_rope_kernel, q_blocks=q_cols // tn, rope_blocks=(q_cols + kv_cols) // tn,
                                          scale=ATT_HEAD_DIM ** -0.5),
                        BF16, tn, extra_in=(cos_t, sin_t), extra_specs=rope_specs)

            q_off = nctb if lat_only else 0

            def qblock(i):
                i = i + q_off
                is_ctx = i < nctb
                return is_ctx, jnp.where(is_ctx, i // nct, (i - nctb) // nlt), jnp.where(is_ctx, 0, (i - nctb) % nlt)

            def win_map(shift, colblk):
                def f(i):
                    is_ctx, b, nblk_ = qblock(i)
                    nb = jnp.clip(nblk_ + shift, 0, nlt - 1)
                    return (nctb + b * nlt + nb, colblk)
                return f

            def ctx_map(colblk):
                def f(i):
                    return (qblock(i)[1], colblk)
                return f

            def kind_map(i):
                is_ctx, b, nblk_ = qblock(i)
                kind = 1 + 2 * (nblk_ == 0).astype(jnp.int32) + (nblk_ == nlt - 1).astype(jnp.int32)
                return (jnp.where(is_ctx, 0, kind), 0, 0)

            kblk = q_cols // kv_cols
            att_rows = m - q_off * CHUNK
            att = pl.pallas_call(
                functools.partial(_attn_kernel, kv_heads=kv_heads),
                grid=(att_rows // CHUNK,),
                in_specs=[
                    pl.BlockSpec(memory_space=pltpu.SMEM),
                    pl.BlockSpec((CHUNK, q_cols), lambda i: (i + q_off, 0)),
                    pl.BlockSpec((CHUNK, kv_cols), win_map(-1, kblk)),
                    pl.BlockSpec((CHUNK, kv_cols), win_map(0, kblk)),
                    pl.BlockSpec((CHUNK, kv_cols), win_map(1, kblk)),
                    pl.BlockSpec((CHUNK, kv_cols), win_map(-1, kblk + 1)),
                    pl.BlockSpec((CHUNK, kv_cols), win_map(0, kblk + 1)),
                    pl.BlockSpec((CHUNK, kv_cols), win_map(1, kblk + 1)),
                    pl.BlockSpec((ctx_len, kv_cols), ctx_map(kblk)),
                    pl.BlockSpec((ctx_len, kv_cols), ctx_map(kblk + 1)),
                    pl.BlockSpec((None, CHUNK, 3 * CHUNK), kind_map),
                ],
                out_specs=pl.BlockSpec((CHUNK, q_cols), lambda i: (i, 0)),
                out_shape=jax.ShapeDtypeStruct((att_rows, q_cols), BF16),
                compiler_params=_params(("parallel",)),
            )(o_sink[i2], p, p, p, p, p, p, p, p, p, bias_t)

            xs, h = fused_resid(layer, 0, (layer, 1), xs, lat_only, a_list=[att], w=o_w_out[i2].astype(BF16))

        xs, h = fused_resid(layer, 1, (layer + 1, 0) if layer + 1 < depth else None, xs, lat_only,
                            h_in=h, w_in=w_ffn_in[layer].astype(BF16), w_out=w_ffn_out[layer].astype(BF16))

    return xs.reshape(batch, seq, d)
```

```python
import functools
import math

import jax
import jax.numpy as jnp
from jax import lax
from jax.experimental import pallas as pl
from jax.experimental.pallas import tpu as pltpu

F32 = jnp.float32
BF16 = jnp.bfloat16

NORM_EPS = 1e-6
NEG_INF = -1e30
GRID_W = 64
ROPE_BASE = 10000.0
SSD_HEAD_DIM = 64
SSD_GROUPS = 2
SSD_STATE = 128
ATT_HEAD_DIM = 128
ATT_GROUP = 4
CHUNK = 128
LANES = 128
DT_PAD = LANES
HALO = 16
VMEM_LIMIT_BYTES = 56 * 1024 * 1024


def _pick(n, pref):
    best = None
    for t in range(LANES, min(n, pref) + 1, LANES):
        if n % t == 0:
            best = t
    assert best is not None, (n, pref)
    return best


def _params(sem):
    return pltpu.CompilerParams(dimension_semantics=sem, vmem_limit_bytes=VMEM_LIMIT_BYTES)


def _rms(x, w):
    return x * lax.rsqrt(jnp.mean(x * x, axis=-1, keepdims=True) + NORM_EPS) * w


def _silu(x):
    return x * jax.nn.sigmoid(x)


def _gelu_tanh(x):
    return 0.5 * x * (1.0 + jnp.tanh(math.sqrt(2.0 / math.pi) * (x + 0.044715 * (x * x * x))))


def _mod_kernel(c_ref, w_ref, b_ref, o_ref):
    s = _silu(c_ref[...]).astype(BF16)
    o_ref[0] = jnp.dot(s, w_ref[0].astype(BF16), preferred_element_type=F32) + b_ref[0]


def _modulation(cond, w_mod, b_mod):
    depth, d, n = w_mod.shape
    rows = cond.shape[0]
    tn = _pick(n, 1024)
    return pl.pallas_call(
        _mod_kernel,
        grid=(depth, n // tn),
        in_specs=[
            pl.BlockSpec((rows, d), lambda l, j: (0, 0)),
            pl.BlockSpec((1, d, tn), lambda l, j: (l, 0, j)),
            pl.BlockSpec((1, 1, tn), lambda l, j: (l, 0, j)),
        ],
        out_specs=pl.BlockSpec((1, rows, tn), lambda l, j: (l, 0, j)),
        out_shape=jax.ShapeDtypeStruct((depth, rows, n), F32),
        compiler_params=_params(("arbitrary", "arbitrary")),
    )(cond, w_mod, b_mod.reshape(depth, 1, n))


def _stream_norm_kernel(c_ref, x_ref, nw_ref, sc_ref, sh_ref, xs_ref, h_ref, *, ctx_blocks):
    xv = jnp.where(pl.program_id(0) < ctx_blocks, c_ref[...], x_ref[...])
    xs_ref[...] = xv
    h_ref[...] = (_rms(xv, nw_ref[...]) * (1.0 + sc_ref[...]) + sh_ref[...]).astype(BF16)


def _mm_dt_kernel(a_ref, w_ref, wdt_ref, b_ref, o_ref, dt_ref):
    a = a_ref[...]
    o_ref[...] = jnp.dot(a, w_ref[...], preferred_element_type=F32).astype(o_ref.dtype)

    @pl.when(pl.program_id(1) == 0)
    def _():
        p = jnp.dot(a, wdt_ref[...], preferred_element_type=F32) + b_ref[...]
        dt_ref[...] = jnp.maximum(p, 0.0) + jnp.log1p(jnp.exp(-jnp.abs(p)))


def _mm_rope_kernel(a_ref, w_ref, cos_ref, sin_ref, o_ref, *, q_blocks, rope_blocks, scale):
    j = pl.program_id(1)
    acc = jnp.dot(a_ref[...], w_ref[...], preferred_element_type=F32)
    s = jnp.where(j < q_blocks, scale, 1.0).astype(F32)
    rot = j < rope_blocks
    cos = jnp.where(rot, cos_ref[...], 1.0) * s
    sin = jnp.where(rot, sin_ref[...], 0.0) * s
    for c in range(acc.shape[1] // ATT_HEAD_DIM):
        xc = acc[:, c * ATT_HEAD_DIM:(c + 1) * ATT_HEAD_DIM]
        o_ref[:, c * ATT_HEAD_DIM:(c + 1) * ATT_HEAD_DIM] = (
            xc * cos + pltpu.roll(xc, ATT_HEAD_DIM // 2, 1) * sin).astype(o_ref.dtype)


def _resid_epilogue(o, x_ref, g_ref, nw_ref, nxt, xo_ref, ho_ref):
    xn = x_ref[...] + g_ref[...] * _rms(o, nw_ref[...])
    xo_ref[...] = xn
    if ho_ref is not None:
        nwn_ref, sc_ref, sh_ref = nxt
        ho_ref[...] = (_rms(xn, nwn_ref[...]) * (1.0 + sc_ref[...]) + sh_ref[...]).astype(BF16)


def _fused_resid_kernel(*refs, mlp, n_a, nblk, ne, se, has_next):
    if mlp:
        h_ref, wg_ref, wu_ref, wo_ref = refs[:4]
        pos = 4
    else:
        a_refs = refs[:n_a]
        w_ref = refs[n_a]
        pos = n_a + 1
    x_ref, g_ref, nw_ref = refs[pos:pos + 3]
    pos += 3
    nxt = refs[pos:pos + 3] if has_next else None
    pos += 3 if has_next else 0
    xo_ref = refs[pos]
    ho_ref = refs[pos + 1] if has_next else None
    accs = refs[-2:]
    nslab, tn = accs[0].shape[0], accs[0].shape[2]
    i = pl.program_id(0)
    j = pl.program_id(1)

    def produce(acc_ref):
        if mlp:
            a = h_ref[...]
            g = jnp.dot(a, wg_ref[...], preferred_element_type=F32)
            u = jnp.dot(a, wu_ref[...], preferred_element_type=F32)
            act = (_silu(g) * u).astype(BF16)
            for n in range(nslab):
                part = jnp.dot(act, wo_ref[:, n * tn:(n + 1) * tn], preferred_element_type=F32)
                acc_ref[n] = jnp.where(j == 0, 0.0, acc_ref[n]) + part
        else:
            val, k0 = None, 0
            for a_ref in a_refs:
                ka = a_ref.shape[1]
                part = jnp.dot(a_ref[...], w_ref[j, k0:k0 + ka, :], preferred_element_type=F32)
                val = part if val is None else val + part
                k0 += ka
            acc_ref[j] = val

    def epilogue(acc_ref):
        r0 = pl.multiple_of(jnp.minimum(j, ne - 1) * se, se)
        o = jnp.concatenate([acc_ref[n, pl.ds(r0, se), :] for n in range(nslab)], axis=1)
        _resid_epilogue(o, x_ref, g_ref, nw_ref, nxt, xo_ref, ho_ref)

    @pl.when(jnp.logical_and(i == 0, j == 0))
    def _():
        for acc_ref in accs:
            acc_ref[...] = jnp.zeros_like(acc_ref)

    @pl.when(i == 0)
    def _():
        produce(accs[0])

    for parity in (0, 1):

        @pl.when(jnp.logical_and(jnp.logical_and(i > 0, i < nblk), i % 2 == parity))
        def _(parity=parity):
            epilogue(accs[1 - parity])
            produce(accs[parity])

    @pl.when(jnp.logical_and(i == nblk, j < ne))
    def _():
        epilogue(accs[(nblk - 1) % 2])


def _conv_kernel(prev_ref, cur_ref, next_ref, w_ref, b_ref, o_ref, ext_ref, *, rows, taps, ctx_blocks,
                 ctx_per_seq, lat_per_seq):
    i = pl.program_id(0)
    in_ctx = i < ctx_blocks
    local = jnp.where(in_ctx, i % ctx_per_seq, (i - ctx_blocks) % lat_per_seq)
    per_seq = jnp.where(in_ctx, ctx_per_seq, lat_per_seq)
    first = local == 0
    last = local == per_seq - 1
    ext_ref[0:HALO, :] = jnp.where(first, 0.0, prev_ref[...].astype(F32))
    ext_ref[HALO:HALO + rows, :] = cur_ref[...].astype(F32)
    ext_ref[HALO + rows:HALO + rows + HALO, :] = jnp.where(last, 0.0, next_ref[...].astype(F32))
    acc = jnp.zeros(o_ref.shape, F32) + b_ref[...]
    for k in range(taps):
        acc = acc + w_ref[k:k + 1, :] * ext_ref[pl.ds(HALO - taps // 2 + k, rows), :]
    o_ref[...] = _silu(acc).astype(o_ref.dtype)


def _cumsum_rows(x):
    rows = x.shape[0]
    row = lax.broadcasted_iota(jnp.int32, x.shape, 0)
    k = 1
    while k < rows:
        x = x + jnp.where(row >= k, pltpu.roll(x, k, 0), 0.0)
        k *= 2
    return x


def _ssd_chunk(rev, cc, xbc_ref, dt_ref, z_ref, alog_ref, dskip_ref, nw_ref, o_ref, state_ref, yacc_ref, *,
               heads, inner):
    q = CHUNK
    hpg = heads // SSD_GROUPS
    gw = hpg * SSD_HEAD_DIM
    n = SSD_STATE
    d = 1 if rev else 0

    dtf = dt_ref[...]
    da = dtf * (-jnp.exp(alog_ref[...]))
    cs = _cumsum_rows(da)
    tot = cs[q - 1:q, :]
    acs = (tot - cs + da) if rev else cs
    acs_t = acs.T
    dt_t = dtf.T
    tot_t = acs_t[:, 0:1] if rev else acs_t[:, q - 1:q]
    w_state_t = jnp.exp(tot_t - acs_t) * dt_t
    w_off = jnp.exp(acs)
    e_tot = jnp.exp(tot)

    li = lax.broadcasted_iota(jnp.int32, (q, q), 0)
    si = lax.broadcasted_iota(jnp.int32, (q, q), 1)
    valid = (li <= si) if rev else (li >= si)
    lane = lax.broadcasted_iota(jnp.int32, (q, LANES), 1)
    lo = lane < SSD_HEAD_DIM

    for g in range(SSD_GROUPS):
        bg = xbc_ref[:, inner + g * n:inner + (g + 1) * n]
        cg = xbc_ref[:, inner + (SSD_GROUPS + g) * n:inner + (SSD_GROUPS + g + 1) * n]
        cb = lax.dot_general(cg, bg, (((1,), (1,)), ((), ())), preferred_element_type=F32)
        bg_t = bg.astype(F32).T
        s_old = state_ref[g]
        y_off = jnp.dot(cg, s_old.astype(BF16), preferred_element_type=F32)
        for pr in range(hpg // 2):
            h1 = g * hpg + 2 * pr
            l1 = d * heads + h1
            l2 = l1 + 1
            cols = slice(h1 * SSD_HEAD_DIM, h1 * SSD_HEAD_DIM + LANES)
            pcols = slice(pr * LANES, (pr + 1) * LANES)

            def decay_mat(l):
                seg = acs[:, l:l + 1] - acs_t[l:l + 1, :]
                return cb * jnp.exp(jnp.where(valid, seg, NEG_INF)) * dt_t[l:l + 1, :]

            lhs = jnp.concatenate([decay_mat(l1), decay_mat(l2)], axis=1).astype(BF16)
            xp = xbc_ref[:, cols].astype(F32)
            rhs = jnp.concatenate([jnp.where(lo, xp, 0.0), jnp.where(lo, 0.0, xp)],
                                  axis=0).astype(BF16)
            y_diag = jnp.dot(lhs, rhs, preferred_element_type=F32)
            w_pair = jnp.where(lo, w_off[:, l1:l1 + 1], w_off[:, l2:l2 + 1])
            y_pair = y_diag + y_off[:, pcols] * w_pair
            lhs_s = jnp.concatenate([bg_t * w_state_t[l1:l1 + 1, :], bg_t * w_state_t[l2:l2 + 1, :]],
                                    axis=1).astype(BF16)
            s_new = jnp.dot(lhs_s, rhs, preferred_element_type=F32)
            dec = jnp.where(lo[0:1, :], e_tot[:, l1:l1 + 1], e_tot[:, l2:l2 + 1])
            state_ref[g, :, pcols] = s_old[:, pcols] * dec + s_new
            if rev:
                yacc_ref[cc, :, cols] = y_pair
            else:
                yacc_ref[cc, :, cols] = yacc_ref[cc, :, cols] + y_pair

    if not rev:
        y = yacc_ref[cc] + xbc_ref[:, 0:inner].astype(F32) * dskip_ref[...]
        y = y * _silu(z_ref[...].astype(F32))
        for g in range(SSD_GROUPS):
            yg = y[:, g * gw:(g + 1) * gw]
            o_ref[:, g * gw:(g + 1) * gw] = _rms(yg, nw_ref[:, g * gw:(g + 1) * gw]).astype(o_ref.dtype)


def _ssd_kernel(xbc_ref, dt_ref, z_ref, alog_ref, dskip_ref, nw_ref, o_ref, state_ref, yacc_ref, *, nct, nlt,
                heads, inner):
    ph = pl.program_id(1)
    c = pl.program_id(2)

    @pl.when(c == 0)
    def _():
        state_ref[...] = jnp.zeros_like(state_ref)

    rev_cc = jnp.where(c < nct, nct - 1 - c, 2 * nct + nlt - 1 - c)
    for rev in (True, False):

        @pl.when(ph == (0 if rev else 1))
        def _(rev=rev):
            _ssd_chunk(rev, rev_cc if rev else c, xbc_ref, dt_ref, z_ref, alog_ref, dskip_ref, nw_ref, o_ref,
                       state_ref, yacc_ref, heads=heads, inner=inner)


def _sgu_kernel(u_ref, v_ref, w_ref, bias_ref, o_ref, *, chunks, groups):
    for ch in range(chunks):
        rows = slice(ch * CHUNK, (ch + 1) * CHUNK)
        gv = _gelu_tanh(v_ref[rows, :].astype(F32))
        mu = jnp.mean(gv, axis=-1, keepdims=True)
        dv = gv - mu
        vn = (dv * lax.rsqrt(jnp.mean(dv * dv, axis=-1, keepdims=True) + NORM_EPS)).astype(BF16)
        gu = _gelu_tanh(u_ref[rows, :].astype(F32))
        for g in range(groups):
            cols = slice(g * LANES, (g + 1) * LANES)
            mixed = jnp.dot(w_ref[g], vn[:, cols], preferred_element_type=F32) + bias_ref[:, cols]
            o_ref[rows, cols] = (gu[:, cols] * mixed).astype(o_ref.dtype)


def _attn_kernel(sink_ref, q_ref, kp_ref, kc_ref, kn_ref, vp_ref, vc_ref, vn_ref, kx_ref, vx_ref, bias_ref,
                 o_ref, *, kv_heads):
    hd = ATT_HEAD_DIM
    qb = CHUNK
    nwin = 3 * CHUNK
    bias = jnp.concatenate([bias_ref[...]] * ATT_GROUP, axis=0)
    row = lax.broadcasted_iota(jnp.int32, (ATT_GROUP * qb, 1), 0)
    nkeys = nwin + kx_ref.shape[0]
    ones = jnp.ones((nkeys, hd), BF16)
    for kk in range(kv_heads):
        kcols = slice(kk * hd, (kk + 1) * hd)
        k_all = jnp.concatenate([kp_ref[:, kcols], kc_ref[:, kcols], kn_ref[:, kcols], kx_ref[:, kcols]], axis=0)
        v_all = jnp.concatenate([vp_ref[:, kcols], vc_ref[:, kcols], vn_ref[:, kcols], vx_ref[:, kcols]], axis=0)
        q_all = jnp.concatenate(
            [q_ref[:, (kk * ATT_GROUP + g) * hd:(kk * ATT_GROUP + g + 1) * hd] for g in range(ATT_GROUP)], axis=0)
        s = lax.dot_general(q_all, k_all, (((1,), (1,)), ((), ())), preferred_element_type=F32)
        s = jnp.concatenate([s[:, :nwin] + bias, s[:, nwin:]], axis=1)
        sink = jnp.full((ATT_GROUP * qb, 1), sink_ref[kk * ATT_GROUP], F32)
        for g in range(1, ATT_GROUP):
            sink = jnp.where(row >= g * qb, sink_ref[kk * ATT_GROUP + g], sink)
        m = jnp.maximum(jnp.max(s, axis=-1, keepdims=True), sink)
        p = jnp.exp(s - m).astype(BF16)
        ov = jnp.dot(p, jnp.concatenate([v_all, ones], axis=1), preferred_element_type=F32)
        denom = ov[:, hd:hd + 1] + jnp.exp(sink - m)
        o = ov[:, :hd] * (1.0 / denom)
        for g in range(ATT_GROUP):
            h = kk * ATT_GROUP + g
            o_ref[:, h * hd:(h + 1) * hd] = o[g * qb:(g + 1) * qb, :].astype(o_ref.dtype)


def _rope_tables(seq, tm):
    pos = jnp.arange(seq, dtype=jnp.int32)
    row = (pos // GRID_W).astype(F32)
    col = (pos % GRID_W).astype(F32)
    quarter = ATT_HEAD_DIM // 4
    inv = ROPE_BASE ** (-jnp.arange(quarter, dtype=F32) / quarter)
    ang = jnp.stack([row[:, None] * inv, col[:, None] * inv], axis=1).reshape(seq, 2 * quarter)
    cos = jnp.concatenate([jnp.cos(ang), jnp.cos(ang)], axis=1)
    sin = jnp.concatenate([-jnp.sin(ang), jnp.sin(ang)], axis=1)
    cos = jnp.concatenate([jnp.ones((tm, ATT_HEAD_DIM), F32), cos], axis=0)
    sin = jnp.concatenate([jnp.zeros((tm, ATT_HEAD_DIM), F32), sin], axis=0)
    return cos, sin


def _attn_bias_table():
    i = jnp.arange(CHUNK)[:, None]
    j = jnp.arange(CHUNK)[None, :]
    left = j >= i
    right = j <= i
    on = jnp.ones((CHUNK, CHUNK), bool)
    off = jnp.zeros((CHUNK, CHUNK), bool)
    kinds = [
        [off, off, off],
        [left, on, right],
        [left, on, off],
        [off, on, right],
        [off, on, off],
    ]
    masks = jnp.stack([jnp.concatenate(k, axis=1) for k in kinds])
    return jnp.where(masks, 0.0, NEG_INF).astype(F32)


def kernel(x, c, ctx, c_ctx, w_mod, b_mod, norm_w, w_ffn_in, w_ffn_out, e_w_in, e_conv_w, e_conv_b, e_dt_bias,
           e_a_log, e_d_skip, e_ssd_norm_w, e_sgu_w, e_sgu_b, e_w_out, o_w_qkv, o_sink, o_w_out):
    batch, seq, d = x.shape
    ctx_len = ctx.shape[1]
    depth = w_mod.shape[0]
    ffn_hidden = w_ffn_out.shape[1]
    inner = d // 2
    heads = inner // SSD_HEAD_DIM
    hpg = heads // SSD_GROUPS
    xbc_dim = inner + 2 * SSD_GROUPS * SSD_STATE
    sgu_width = d - inner
    sgu_groups = sgu_width // LANES
    taps = e_conv_w.shape[1]
    att_heads = d // ATT_HEAD_DIM
    kv_heads = att_heads // ATT_GROUP
    q_cols = att_heads * ATT_HEAD_DIM
    kv_cols = kv_heads * ATT_HEAD_DIM

    nc = batch * ctx_len
    nl = batch * seq
    m = nc + nl
    nct = ctx_len // CHUNK
    nlt = seq // CHUNK
    assert ctx_len % CHUNK == 0 and seq % CHUNK == 0 and seq % GRID_W == 0
    assert hpg % 2 == 0 and 2 * heads <= DT_PAD and (3 * inner) % xbc_dim == 0 and inner == sgu_width
    assert q_cols == d and taps // 2 <= HALO

    tm = _pick(math.gcd(nc, seq), 1024)
    tr = _pick(math.gcd(nc, seq), 512)
    ncb = nc // tm
    n_mod = 1 + batch
    mod_rows = 8

    def mod_row(i, t):
        return jnp.where(i < nc // t, 0, 1 + (i - nc // t) // (seq // t))

    cond = jnp.concatenate([c_ctx[None, :], c, jnp.zeros((mod_rows - n_mod, d), F32)], axis=0)
    mod = _modulation(cond, w_mod, b_mod).reshape(depth * mod_rows, 1, 6 * d)
    norm_w3 = norm_w.reshape(depth * 4, 1, d)

    def mod_spec(layer, part, blk):
        return pl.BlockSpec((None, 1, d), lambda *g: (layer * mod_rows + mod_row(blk(*g), tm), 0, part))

    def vec_spec(row):
        return pl.BlockSpec((None, 1, d), lambda *g: (row, 0, 0))

    ctb = nc // tr
    xs, h = pl.pallas_call(
        functools.partial(_stream_norm_kernel, ctx_blocks=ctb),
        grid=(m // tr,),
        in_specs=[
            pl.BlockSpec((tr, d), lambda i: (jnp.minimum(i, ctb - 1), 0)),
            pl.BlockSpec((tr, d), lambda i: (jnp.maximum(i - ctb, 0), 0)),
            vec_spec(0),
            mod_spec(0, 1, lambda i: i * tr // tm),
            mod_spec(0, 0, lambda i: i * tr // tm),
        ],
        out_specs=[pl.BlockSpec((tr, d), lambda i: (i, 0)), pl.BlockSpec((tr, d), lambda i: (i, 0))],
        out_shape=[jax.ShapeDtypeStruct((m, d), F32), jax.ShapeDtypeStruct((m, d), BF16)],
        compiler_params=_params(("parallel",)),
    )(ctx.reshape(nc, d), x.reshape(nl, d), norm_w3, mod, mod)

    def project(a, w, wl, kernel_fn, tn, extra_in=(), extra_specs=(), extra_out=(), extra_out_specs=()):
        n_out = w.shape[2]
        res = pl.pallas_call(
            kernel_fn,
            grid=(m // tm, n_out // tn),
            in_specs=[pl.BlockSpec((tm, d), lambda i, j: (i, 0)),
                      pl.BlockSpec((None, d, tn), lambda i, j: (wl, 0, j))] + list(extra_specs),
            out_specs=[pl.BlockSpec((tm, tn), lambda i, j: (i, j))] + list(extra_out_specs),
            out_shape=[jax.ShapeDtypeStruct((m, n_out), BF16)] + list(extra_out),
            compiler_params=_params(("parallel", "arbitrary")),
        )(a, w, *extra_in)
        return res if extra_out else res[0]

    def fused_resid(layer, sub, next_layer, xs_in, lat_only, wl, a_list=None, w=None, h_in=None, w_in=None,
                    w_out=None):
        mlp = a_list is None
        rows = nl if lat_only else m
        nblk = rows // tm
        mod_off = ncb if lat_only else 0
        tn = _pick(d, 512)
        nslab = d // tn
        if mlp:
            tf = _pick(ffn_hidden, 512)
            nsteps = ffn_hidden // tf
        else:
            nsteps = nslab
        ne = 1
        while ne * 2 <= min(nsteps, 8) and (tm // (ne * 2)) % 16 == 0:
            ne *= 2
        se = tm // ne
        has_next = next_layer is not None

        def off(arr):
            return (arr.shape[0] - rows) // tm

        def blk_map(arr):
            return lambda i, j: (jnp.minimum(i, nblk - 1) + off(arr), 0)

        def sub_idx(i, j):
            return jnp.where(i == 0, 0, (i - 1) * ne + jnp.minimum(j, ne - 1))

        def prev_blk(i, j):
            return jnp.maximum(i - 1, 0) + mod_off

        if mlp:
            in_specs = [
                pl.BlockSpec((tm, d), blk_map(h_in)),
                pl.BlockSpec((None, d, tf), lambda i, j: (wl, 0, j)),
                pl.BlockSpec((None, d, tf), lambda i, j: (wl, 0, j + nsteps)),
                pl.BlockSpec((None, tf, d), lambda i, j: (wl, j, 0)),
            ]
            args = [h_in, w_in, w_in, w_out]
        else:
            in_specs = [pl.BlockSpec((tm, a.shape[1]), blk_map(a)) for a in a_list]
            in_specs.append(pl.BlockSpec((None, nslab, w.shape[2], tn), lambda i, j: (wl, 0, 0, 0),
                                         pipeline_mode=pl.Buffered(1)))
            args = list(a_list) + [w]
        x_off = off(xs_in) * ne
        in_specs += [
            pl.BlockSpec((se, d), lambda i, j: (sub_idx(i, j) + x_off, 0)),
            mod_spec(layer, 2 if sub == 0 else 5, prev_blk),
            vec_spec(layer * 4 + (1 if sub == 0 else 3)),
        ]
        args += [xs_in, mod, norm_w3]
        out_specs = [pl.BlockSpec((se, d), lambda i, j: (sub_idx(i, j), 0))]
        out_shape = [jax.ShapeDtypeStruct((rows, d), F32)]
        if has_next:
            nl_, nsub = next_layer
            in_specs += [vec_spec(nl_ * 4 + (0 if nsub == 0 else 2)),
                         mod_spec(nl_, 1 if nsub == 0 else 4, prev_blk),
                         mod_spec(nl_, 0 if nsub == 0 else 3, prev_blk)]
            args += [norm_w3, mod, mod]
            out_specs.append(pl.BlockSpec((se, d), lambda i, j: (sub_idx(i, j), 0)))
            out_shape.append(jax.ShapeDtypeStruct((rows, d), BF16))
        res = pl.pallas_call(
            functools.partial(_fused_resid_kernel, mlp=mlp, n_a=0 if mlp else len(a_list), nblk=nblk, ne=ne, se=se,
                              has_next=has_next),
            grid=(nblk + 1, nsteps),
            in_specs=in_specs,
            out_specs=out_specs,
            out_shape=out_shape,
            scratch_shapes=[pltpu.VMEM((nslab, tm, tn), F32), pltpu.VMEM((nslab, tm, tn), F32)],
            compiler_params=_params(("arbitrary", "arbitrary")),
        )(*args)
        return (res[0], res[1]) if has_next else (res[0], None)

    cos_t, sin_t = _rope_tables(seq, tm)
    bias_t = _attn_bias_table()
    nctb = nc // CHUNK

    def slabs(w):
        tn = _pick(d, 512)
        return w.astype(BF16).reshape(w.shape[0], w.shape[1], d // tn, tn).transpose(0, 2, 1, 3)

    w_ffn_in_b = w_ffn_in.astype(BF16)
    w_ffn_out_b = w_ffn_out.astype(BF16)
    e_w_out_b = slabs(e_w_out)
    o_w_out_b = slabs(o_w_out)
    z_end, xbc_end, dt_end = inner, inner + xbc_dim, inner + xbc_dim + 2 * heads
    w_main_b = jnp.concatenate([e_w_in[..., :z_end], e_w_in[..., dt_end:], e_w_in[..., z_end:xbc_end]],
                               axis=-1).astype(BF16)
    w_dt_b = jnp.pad(e_w_in[..., xbc_end:dt_end], ((0, 0), (0, 0), (0, DT_PAD - 2 * heads))).astype(BF16)
    n_rot = (q_cols + kv_cols) // ATT_HEAD_DIM
    quarter = ATT_HEAD_DIM // 4
    rot_cols = o_w_qkv[..., :q_cols + kv_cols].reshape(-1, d, n_rot, 2, 2, quarter).transpose(0, 1, 2, 4, 3, 5)
    w_qkv_b = jnp.concatenate([rot_cols.reshape(-1, d, q_cols + kv_cols), o_w_qkv[..., q_cols + kv_cols:]],
                              axis=-1).astype(BF16)

    for layer in range(depth):
        i2 = layer // 2
        lat_only = layer == depth - 1
        if layer % 2 == 0:
            dt_bias = jnp.pad(e_dt_bias[i2].reshape(1, 2 * heads), ((0, 0), (0, DT_PAD - 2 * heads)))
            p, dt = project(h, w_main_b, i2, _mm_dt_kernel, _pick(math.gcd(inner, xbc_dim), 512),
                            extra_in=(w_dt_b, dt_bias),
                            extra_specs=[pl.BlockSpec((None, d, DT_PAD), lambda i, j: (i2, 0, 0)),
                                         pl.BlockSpec((1, DT_PAD), lambda i, j: (0, 0))],
                            extra_out=[jax.ShapeDtypeStruct((m, DT_PAD), F32)],
                            extra_out_specs=[pl.BlockSpec((tm, DT_PAD), lambda i, j: (i, 0))])

            cr = _pick(math.gcd(ctx_len, seq), 256)
            xcol = (3 * inner) // xbc_dim
            hb = cr // HALO
            last_hb = m // HALO - 1
            xbc = pl.pallas_call(
                functools.partial(_conv_kernel, rows=cr, taps=taps, ctx_blocks=nc // cr,
                                  ctx_per_seq=ctx_len // cr, lat_per_seq=seq // cr),
                grid=(m // cr,),
                in_specs=[
                    pl.BlockSpec((HALO, xbc_dim), lambda i: (jnp.maximum(i * hb - 1, 0), xcol)),
                    pl.BlockSpec((cr, xbc_dim), lambda i: (i, xcol)),
                    pl.BlockSpec((HALO, xbc_dim), lambda i: (jnp.minimum((i + 1) * hb, last_hb), xcol)),
                    pl.BlockSpec((taps, xbc_dim), lambda i: (0, 0)),
                    pl.BlockSpec((1, xbc_dim), lambda i: (0, 0)),
                ],
                out_specs=pl.BlockSpec((cr, xbc_dim), lambda i: (i, 0)),
                out_shape=jax.ShapeDtypeStruct((m, xbc_dim), BF16),
                scratch_shapes=[pltpu.VMEM((cr + 2 * HALO, xbc_dim), F32)],
                compiler_params=_params(("parallel",)),
            )(p, p, p, e_conv_w[i2], e_conv_b[i2].reshape(1, xbc_dim))

            def chunk_pos(ph, cidx):
                rev_cc = jnp.where(cidx < nct, nct - 1 - cidx, 2 * nct + nlt - 1 - cidx)
                return jnp.where(ph == 0, rev_cc, cidx)

            def row_block(b, cc):
                return jnp.where(cc < nct, b * nct + cc, batch * nct + b * nlt + cc - nct)

            def in_map(b, ph, cidx):
                return (row_block(b, chunk_pos(ph, cidx)), 0)

            def fwd_only_map(b, ph, cidx):
                return (row_block(b, jnp.where(ph == 0, 0, cidx)), 0)

            a_log = jnp.pad(e_a_log[i2].reshape(1, 2 * heads), ((0, 0), (0, DT_PAD - 2 * heads)))
            d_skip = jnp.repeat(e_d_skip[i2], SSD_HEAD_DIM).reshape(1, inner)
            y_ssd = pl.pallas_call(
                functools.partial(_ssd_kernel, nct=nct, nlt=nlt, heads=heads, inner=inner),
                grid=(batch, 2, nct + nlt),
                in_specs=[
                    pl.BlockSpec((CHUNK, xbc_dim), in_map),
                    pl.BlockSpec((CHUNK, DT_PAD), in_map),
                    pl.BlockSpec((CHUNK, inner), fwd_only_map),
                    pl.BlockSpec((1, DT_PAD), lambda b, ph, cidx: (0, 0)),
                    pl.BlockSpec((1, inner), lambda b, ph, cidx: (0, 0)),
                    pl.BlockSpec((1, inner), lambda b, ph, cidx: (0, 0)),
                ],
                out_specs=pl.BlockSpec((CHUNK, inner), fwd_only_map),
                out_shape=jax.ShapeDtypeStruct((m, inner), BF16),
                scratch_shapes=[pltpu.VMEM((SSD_GROUPS, SSD_STATE, hpg * SSD_HEAD_DIM), F32),
                                pltpu.VMEM((nct + nlt, CHUNK, inner), F32)],
                compiler_params=_params(("arbitrary", "arbitrary", "arbitrary")),
            )(xbc, dt, p, a_log, d_skip, e_ssd_norm_w[i2].reshape(1, inner))

            sr = _pick(math.gcd(ctx_len, seq), 256)
            sgu_bias = jnp.repeat(e_sgu_b[i2].T, LANES, axis=1)
            y_sgu = pl.pallas_call(
                functools.partial(_sgu_kernel, chunks=sr // CHUNK, groups=sgu_groups),
                grid=(m // sr,),
                in_specs=[
                    pl.BlockSpec((sr, sgu_width), lambda i: (i, 1)),
                    pl.BlockSpec((sr, sgu_width), lambda i: (i, 2)),
                    pl.BlockSpec((sgu_groups, CHUNK, CHUNK), lambda i: (0, 0, 0)),
                    pl.BlockSpec((CHUNK, sgu_width), lambda i: (0, 0)),
                ],
                out_specs=pl.BlockSpec((sr, sgu_width), lambda i: (i, 0)),
                out_shape=jax.ShapeDtypeStruct((m, sgu_width), BF16),
                compiler_params=_params(("parallel",)),
            )(p, p, e_sgu_w[i2].astype(BF16), sgu_bias)

            xs, h = fused_resid(layer, 0, (layer, 1), xs, lat_only, i2, a_list=[y_ssd, y_sgu], w=e_w_out_b)
        else:
            tn = _pick(kv_cols, 512)
            nblk = seq // tm
            rope_specs = [pl.BlockSpec((tm, ATT_HEAD_DIM),
                                       lambda i, j: (jnp.where(i < ncb, 0, 1 + (i - ncb) % nblk), 0))] * 2
            p = project(h, w_qkv_b, i2,
                        functools.partial(_mm_rope_kernel, q_blocks=q_cols // tn, rope_blocks=(q_cols + kv_cols) // tn,
                                          scale=ATT_HEAD_DIM ** -0.5),
                        tn, extra_in=(cos_t, sin_t), extra_specs=rope_specs)

            q_off = nctb if lat_only else 0

            def qblock(i):
                i = i + q_off
                is_ctx = i < nctb
                return is_ctx, jnp.where(is_ctx, i // nct, (i - nctb) // nlt), jnp.where(is_ctx, 0, (i - nctb) % nlt)

            def win_map(shift, colblk):
                def f(i):
                    is_ctx, b, nblk_ = qblock(i)
                    nb = jnp.clip(nblk_ + shift, 0, nlt - 1)
                    return (nctb + b * nlt + nb, colblk)
                return f

            def ctx_map(colblk):
                def f(i):
                    return (qblock(i)[1], colblk)
                return f

            def kind_map(i):
                is_ctx, b, nblk_ = qblock(i)
                kind = 1 + 2 * (nblk_ == 0).astype(jnp.int32) + (nblk_ == nlt - 1).astype(jnp.int32)
                return (jnp.where(is_ctx, 0, kind), 0, 0)

            kblk = q_cols // kv_cols
            att_rows = m - q_off * CHUNK
            att = pl.pallas_call(
                functools.partial(_attn_kernel, kv_heads=kv_heads),
                grid=(att_rows // CHUNK,),
                in_specs=[
                    pl.BlockSpec(memory_space=pltpu.SMEM),
                    pl.BlockSpec((CHUNK, q_cols), lambda i: (i + q_off, 0)),
                    pl.BlockSpec((CHUNK, kv_cols), win_map(-1, kblk)),
                    pl.BlockSpec((CHUNK, kv_cols), win_map(0, kblk)),
                    pl.BlockSpec((CHUNK, kv_cols), win_map(1, kblk)),
                    pl.BlockSpec((CHUNK, kv_cols), win_map(-1, kblk + 1)),
                    pl.BlockSpec((CHUNK, kv_cols), win_map(0, kblk + 1)),
                    pl.BlockSpec((CHUNK, kv_cols), win_map(1, kblk + 1)),
                    pl.BlockSpec((ctx_len, kv_cols), ctx_map(kblk)),
                    pl.BlockSpec((ctx_len, kv_cols), ctx_map(kblk + 1)),
                    pl.BlockSpec((None, CHUNK, 3 * CHUNK), kind_map),
                ],
                out_specs=pl.BlockSpec((CHUNK, q_cols), lambda i: (i, 0)),
                out_shape=jax.ShapeDtypeStruct((att_rows, q_cols), BF16),
                compiler_params=_params(("parallel",)),
            )(o_sink[i2], p, p, p, p, p, p, p, p, p, bias_t)

            xs, h = fused_resid(layer, 0, (layer, 1), xs, lat_only, i2, a_list=[att], w=o_w_out_b)

        xs, h = fused_resid(layer, 1, (layer + 1, 0) if layer + 1 < depth else None, xs, lat_only, layer,
                            h_in=h, w_in=w_ffn_in_b, w_out=w_ffn_out_b)

    return xs.reshape(batch, seq, d)
```

```python
import functools
import math

import jax
import jax.numpy as jnp
from jax import lax
from jax.experimental import pallas as pl
from jax.experimental.pallas import tpu as pltpu

F32 = jnp.float32
BF16 = jnp.bfloat16

NORM_EPS = 1e-6
NEG_INF = -1e30
GRID_W = 64
ROPE_BASE = 10000.0
SSD_HEAD_DIM = 64
SSD_GROUPS = 2
SSD_STATE = 128
ATT_HEAD_DIM = 128
ATT_GROUP = 4
CHUNK = 128
LANES = 128
DT_PAD = LANES
HALO = 16
VMEM_LIMIT_BYTES = 56 * 1024 * 1024


def _pick(n, pref):
    best = None
    for t in range(LANES, min(n, pref) + 1, LANES):
        if n % t == 0:
            best = t
    assert best is not None, (n, pref)
    return best


def _params(sem):
    return pltpu.CompilerParams(dimension_semantics=sem, vmem_limit_bytes=VMEM_LIMIT_BYTES)


def _rms(x, w):
    return x * lax.rsqrt(jnp.mean(x * x, axis=-1, keepdims=True) + NORM_EPS) * w


def _silu(x):
    return x * jax.nn.sigmoid(x)


def _gelu_tanh(x):
    return 0.5 * x * (1.0 + jnp.tanh(math.sqrt(2.0 / math.pi) * (x + 0.044715 * (x * x * x))))


def _mod_kernel(c_ref, w_ref, b_ref, o_ref):
    s = _silu(c_ref[...]).astype(BF16)
    o_ref[0] = jnp.dot(s, w_ref[0].astype(BF16), preferred_element_type=F32) + b_ref[0]


def _modulation(cond, w_mod, b_mod):
    depth, d, n = w_mod.shape
    rows = cond.shape[0]
    tn = _pick(n, 1024)
    return pl.pallas_call(
        _mod_kernel,
        grid=(depth, n // tn),
        in_specs=[
            pl.BlockSpec((rows, d), lambda l, j: (0, 0)),
            pl.BlockSpec((1, d, tn), lambda l, j: (l, 0, j)),
            pl.BlockSpec((1, 1, tn), lambda l, j: (l, 0, j)),
        ],
        out_specs=pl.BlockSpec((1, rows, tn), lambda l, j: (l, 0, j)),
        out_shape=jax.ShapeDtypeStruct((depth, rows, n), F32),
        compiler_params=_params(("arbitrary", "arbitrary")),
    )(cond, w_mod, b_mod.reshape(depth, 1, n))


def _stream_norm_kernel(c_ref, x_ref, nw_ref, sc_ref, sh_ref, xs_ref, h_ref, *, ctx_blocks):
    xv = jnp.where(pl.program_id(0) < ctx_blocks, c_ref[...], x_ref[...])
    xs_ref[...] = xv
    h_ref[...] = (_rms(xv, nw_ref[...]) * (1.0 + sc_ref[...]) + sh_ref[...]).astype(BF16)


def _mm_dt_kernel(a_ref, w_ref, wdt_ref, b_ref, o_ref, dt_ref):
    a = a_ref[...]
    o_ref[...] = jnp.dot(a, w_ref[...], preferred_element_type=F32).astype(o_ref.dtype)

    @pl.when(pl.program_id(1) == 0)
    def _():
        p = jnp.dot(a, wdt_ref[...], preferred_element_type=F32) + b_ref[...]
        dt_ref[...] = jnp.maximum(p, 0.0) + jnp.log1p(jnp.exp(-jnp.abs(p)))


def _mm_rope_kernel(a_ref, w_ref, cos_ref, sin_ref, o_ref, *, q_blocks, rope_blocks, scale):
    j = pl.program_id(1)
    acc = jnp.dot(a_ref[...], w_ref[...], preferred_element_type=F32)
    s = jnp.where(j < q_blocks, scale, 1.0).astype(F32)
    rot = j < rope_blocks
    cos = jnp.where(rot, cos_ref[...], 1.0) * s
    sin = jnp.where(rot, sin_ref[...], 0.0) * s
    for c in range(acc.shape[1] // ATT_HEAD_DIM):
        xc = acc[:, c * ATT_HEAD_DIM:(c + 1) * ATT_HEAD_DIM]
        o_ref[:, c * ATT_HEAD_DIM:(c + 1) * ATT_HEAD_DIM] = (
            xc * cos + pltpu.roll(xc, ATT_HEAD_DIM // 2, 1) * sin).astype(o_ref.dtype)


def _resid_epilogue(o, x_ref, g_ref, nw_ref, nxt, xo_ref, ho_ref):
    xn = x_ref[...] + g_ref[...] * _rms(o, nw_ref[...])
    xo_ref[...] = xn
    if ho_ref is not None:
        nwn_ref, sc_ref, sh_ref = nxt
        ho_ref[...] = (_rms(xn, nwn_ref[...]) * (1.0 + sc_ref[...]) + sh_ref[...]).astype(BF16)


def _fused_resid_kernel(*refs, mlp, n_a, nblk, ne, se, has_next):
    if mlp:
        h_ref, wg_ref, wu_ref, wo_ref = refs[:4]
        pos = 4
    else:
        a_refs = refs[:n_a]
        w_ref = refs[n_a]
        pos = n_a + 1
    x_ref, g_ref, nw_ref = refs[pos:pos + 3]
    pos += 3
    nxt = refs[pos:pos + 3] if has_next else None
    pos += 3 if has_next else 0
    xo_ref = refs[pos]
    ho_ref = refs[pos + 1] if has_next else None
    accs = refs[-2:]
    nslab, tn = accs[0].shape[0], accs[0].shape[2]
    i = pl.program_id(0)
    j = pl.program_id(1)

    def produce(acc_ref):
        if mlp:
            a = h_ref[...]
            tf = wg_ref.shape[1]
            th = tf // 2
            acts = []
            for c0 in (0, th):
                g = jnp.dot(a, wg_ref[:, c0:c0 + th], preferred_element_type=F32)
                u = jnp.dot(a, wu_ref[:, c0:c0 + th], preferred_element_type=F32)
                acts.append((_silu(g) * u).astype(BF16))
            for n in range(nslab):
                part = jnp.dot(acts[0], wo_ref[0:th, n * tn:(n + 1) * tn], preferred_element_type=F32)
                part = part + jnp.dot(acts[1], wo_ref[th:tf, n * tn:(n + 1) * tn], preferred_element_type=F32)
                acc_ref[n] = jnp.where(j == 0, 0.0, acc_ref[n]) + part
        else:
            val, k0 = None, 0
            for a_ref in a_refs:
                ka = a_ref.shape[1]
                part = jnp.dot(a_ref[...], w_ref[j, k0:k0 + ka, :], preferred_element_type=F32)
                val = part if val is None else val + part
                k0 += ka
            acc_ref[j] = val

    def epilogue(acc_ref):
        r0 = pl.multiple_of(jnp.minimum(j, ne - 1) * se, se)
        o = jnp.concatenate([acc_ref[n, pl.ds(r0, se), :] for n in range(nslab)], axis=1)
        _resid_epilogue(o, x_ref, g_ref, nw_ref, nxt, xo_ref, ho_ref)

    @pl.when(jnp.logical_and(i == 0, j == 0))
    def _():
        for acc_ref in accs:
            acc_ref[...] = jnp.zeros_like(acc_ref)

    @pl.when(i == 0)
    def _():
        produce(accs[0])

    for parity in (0, 1):

        @pl.when(jnp.logical_and(jnp.logical_and(i > 0, i < nblk), i % 2 == parity))
        def _(parity=parity):
            produce(accs[parity])
            epilogue(accs[1 - parity])

    @pl.when(jnp.logical_and(i == nblk, j < ne))
    def _():
        epilogue(accs[(nblk - 1) % 2])


def _conv_kernel(prev_ref, cur_ref, next_ref, w_ref, b_ref, o_ref, ext_ref, *, rows, taps, ctx_blocks,
                 ctx_per_seq, lat_per_seq):
    i = pl.program_id(0)
    in_ctx = i < ctx_blocks
    local = jnp.where(in_ctx, i % ctx_per_seq, (i - ctx_blocks) % lat_per_seq)
    per_seq = jnp.where(in_ctx, ctx_per_seq, lat_per_seq)
    first = local == 0
    last = local == per_seq - 1
    ext_ref[0:HALO, :] = jnp.where(first, 0.0, prev_ref[...].astype(F32))
    ext_ref[HALO:HALO + rows, :] = cur_ref[...].astype(F32)
    ext_ref[HALO + rows:HALO + rows + HALO, :] = jnp.where(last, 0.0, next_ref[...].astype(F32))
    acc = jnp.zeros(o_ref.shape, F32) + b_ref[...]
    for k in range(taps):
        acc = acc + w_ref[k:k + 1, :] * ext_ref[pl.ds(HALO - taps // 2 + k, rows), :]
    o_ref[...] = _silu(acc).astype(o_ref.dtype)


def _cumsum_rows(x):
    rows = x.shape[0]
    row = lax.broadcasted_iota(jnp.int32, x.shape, 0)
    k = 1
    while k < rows:
        x = x + jnp.where(row >= k, pltpu.roll(x, k, 0), 0.0)
        k *= 2
    return x


def _ssd_chunk(rev, cc, rows, xbc_ref, dt_ref, z_ref, alog_ref, dskip_ref, nw_ref, o_ref, state_ref, yacc_ref, *,
               heads, inner):
    q = CHUNK
    hpg = heads // SSD_GROUPS
    gw = hpg * SSD_HEAD_DIM
    n = SSD_STATE
    d = 1 if rev else 0

    dtf = dt_ref[rows, :]
    da = dtf * (-jnp.exp(alog_ref[...]))
    cs = _cumsum_rows(da)
    tot = cs[q - 1:q, :]
    acs = (tot - cs + da) if rev else cs
    acs_t = acs.T
    dt_t = dtf.T
    tot_t = acs_t[:, 0:1] if rev else acs_t[:, q - 1:q]
    w_state_t = jnp.exp(tot_t - acs_t) * dt_t
    w_off = jnp.exp(acs)
    e_tot = jnp.exp(tot)

    li = lax.broadcasted_iota(jnp.int32, (q, q), 0)
    si = lax.broadcasted_iota(jnp.int32, (q, q), 1)
    valid = (li <= si) if rev else (li >= si)
    lane = lax.broadcasted_iota(jnp.int32, (q, LANES), 1)
    lo = lane < SSD_HEAD_DIM

    for g in range(SSD_GROUPS):
        bg = xbc_ref[rows, inner + g * n:inner + (g + 1) * n]
        cg = xbc_ref[rows, inner + (SSD_GROUPS + g) * n:inner + (SSD_GROUPS + g + 1) * n]
        cb = lax.dot_general(cg, bg, (((1,), (1,)), ((), ())), preferred_element_type=F32)
        bg_t = bg.astype(F32).T
        s_old = state_ref[g]
        y_off = jnp.dot(cg, s_old.astype(BF16), preferred_element_type=F32)
        for pr in range(hpg // 2):
            h1 = g * hpg + 2 * pr
            l1 = d * heads + h1
            l2 = l1 + 1
            cols = slice(h1 * SSD_HEAD_DIM, h1 * SSD_HEAD_DIM + LANES)
            pcols = slice(pr * LANES, (pr + 1) * LANES)

            def decay_mat(l):
                seg = acs[:, l:l + 1] - acs_t[l:l + 1, :]
                return cb * jnp.exp(jnp.where(valid, seg, NEG_INF)) * dt_t[l:l + 1, :]

            lhs = jnp.concatenate([decay_mat(l1), decay_mat(l2)], axis=1).astype(BF16)
            xp = xbc_ref[rows, cols].astype(F32)
            rhs = jnp.concatenate([jnp.where(lo, xp, 0.0), jnp.where(lo, 0.0, xp)],
                                  axis=0).astype(BF16)
            y_diag = jnp.dot(lhs, rhs, preferred_element_type=F32)
            w_pair = jnp.where(lo, w_off[:, l1:l1 + 1], w_off[:, l2:l2 + 1])
            y_pair = y_diag + y_off[:, pcols] * w_pair
            lhs_s = jnp.concatenate([bg_t * w_state_t[l1:l1 + 1, :], bg_t * w_state_t[l2:l2 + 1, :]],
                                    axis=1).astype(BF16)
            s_new = jnp.dot(lhs_s, rhs, preferred_element_type=F32)
            dec = jnp.where(lo[0:1, :], e_tot[:, l1:l1 + 1], e_tot[:, l2:l2 + 1])
            state_ref[g, :, pcols] = s_old[:, pcols] * dec + s_new
            if rev:
                yacc_ref[cc, :, cols] = y_pair
            else:
                yacc_ref[cc, :, cols] = yacc_ref[cc, :, cols] + y_pair

    if not rev:
        y = yacc_ref[cc] + xbc_ref[rows, 0:inner].astype(F32) * dskip_ref[...]
        y = y * _silu(z_ref[rows, :].astype(F32))
        for g in range(SSD_GROUPS):
            yg = y[:, g * gw:(g + 1) * gw]
            o_ref[rows, g * gw:(g + 1) * gw] = _rms(yg, nw_ref[:, g * gw:(g + 1) * gw]).astype(o_ref.dtype)


def _ssd_kernel(xbc_ref, dt_ref, z_ref, alog_ref, dskip_ref, nw_ref, o_ref, state_ref, yacc_ref, *, nctb, nltb,
                sb, heads, inner):
    ph = pl.program_id(1)
    c = pl.program_id(2)

    @pl.when(c == 0)
    def _():
        state_ref[...] = jnp.zeros_like(state_ref)

    rev_blk = jnp.where(c < nctb, nctb - 1 - c, 2 * nctb + nltb - 1 - c)
    for rev in (True, False):

        @pl.when(ph == (0 if rev else 1))
        def _(rev=rev):
            blk = rev_blk if rev else c
            for sub in (range(sb - 1, -1, -1) if rev else range(sb)):
                _ssd_chunk(rev, blk * sb + sub, slice(sub * CHUNK, (sub + 1) * CHUNK), xbc_ref, dt_ref, z_ref,
                           alog_ref, dskip_ref, nw_ref, o_ref, state_ref, yacc_ref, heads=heads, inner=inner)


def _sgu_kernel(u_ref, v_ref, w_ref, bias_ref, o_ref, *, chunks, groups):
    for ch in range(chunks):
        rows = slice(ch * CHUNK, (ch + 1) * CHUNK)
        gv = _gelu_tanh(v_ref[rows, :].astype(F32))
        mu = jnp.mean(gv, axis=-1, keepdims=True)
        dv = gv - mu
        vn = (dv * lax.rsqrt(jnp.mean(dv * dv, axis=-1, keepdims=True) + NORM_EPS)).astype(BF16)
        gu = _gelu_tanh(u_ref[rows, :].astype(F32))
        for g in range(groups):
            cols = slice(g * LANES, (g + 1) * LANES)
            mixed = jnp.dot(w_ref[g], vn[:, cols], preferred_element_type=F32) + bias_ref[:, cols]
            o_ref[rows, cols] = (gu[:, cols] * mixed).astype(o_ref.dtype)


def _attn_kernel(sink_ref, q_ref, kp_ref, kc_ref, kn_ref, vp_ref, vc_ref, vn_ref, kx_ref, vx_ref, bias_ref,
                 o_ref, *, kv_heads, split):
    hd = ATT_HEAD_DIM
    qb = CHUNK
    nwin = 3 * CHUNK
    gs = ATT_GROUP // split
    bias = jnp.concatenate([bias_ref[...]] * gs, axis=0)
    row = lax.broadcasted_iota(jnp.int32, (gs * qb, 1), 0)
    nkeys = nwin + kx_ref.shape[0]
    ones = jnp.ones((nkeys, hd), BF16)

    def scores(u):
        kk, h0 = u // split, (u // split) * ATT_GROUP + (u % split) * gs
        kcols = slice(kk * hd, (kk + 1) * hd)
        k_all = jnp.concatenate([kp_ref[:, kcols], kc_ref[:, kcols], kn_ref[:, kcols], kx_ref[:, kcols]], axis=0)
        q_all = jnp.concatenate([q_ref[:, (h0 + g) * hd:(h0 + g + 1) * hd] for g in range(gs)], axis=0)
        return lax.dot_general(q_all, k_all, (((1,), (1,)), ((), ())), preferred_element_type=F32)

    def softmax(u, s):
        h0 = (u // split) * ATT_GROUP + (u % split) * gs
        s = jnp.concatenate([s[:, :nwin] + bias, s[:, nwin:]], axis=1)
        sink = jnp.full((gs * qb, 1), sink_ref[h0], F32)
        for g in range(1, gs):
            sink = jnp.where(row >= g * qb, sink_ref[h0 + g], sink)
        m = jnp.maximum(jnp.max(s, axis=-1, keepdims=True), sink)
        return jnp.exp(s - m).astype(BF16), jnp.exp(sink - m)

    def values(u, p, sink_p):
        kk, h0 = u // split, (u // split) * ATT_GROUP + (u % split) * gs
        kcols = slice(kk * hd, (kk + 1) * hd)
        v_all = jnp.concatenate([vp_ref[:, kcols], vc_ref[:, kcols], vn_ref[:, kcols], vx_ref[:, kcols]], axis=0)
        ov = jnp.dot(p, jnp.concatenate([v_all, ones], axis=1), preferred_element_type=F32)
        o = ov[:, :hd] * (1.0 / (ov[:, hd:hd + 1] + sink_p))
        for g in range(gs):
            o_ref[:, (h0 + g) * hd:(h0 + g + 1) * hd] = o[g * qb:(g + 1) * qb, :].astype(o_ref.dtype)

    units = kv_heads * split
    s_q, p_q = {}, {}
    for t in range(units + 2):
        if t < units:
            s_q[t] = scores(t)
        if 0 <= t - 2 < units:
            values(t - 2, *p_q.pop(t - 2))
        if 0 <= t - 1 < units:
            p_q[t - 1] = softmax(t - 1, s_q.pop(t - 1))


def _rope_tables(seq, tm):
    pos = jnp.arange(seq, dtype=jnp.int32)
    row = (pos // GRID_W).astype(F32)
    col = (pos % GRID_W).astype(F32)
    quarter = ATT_HEAD_DIM // 4
    inv = ROPE_BASE ** (-jnp.arange(quarter, dtype=F32) / quarter)
    ang = jnp.stack([row[:, None] * inv, col[:, None] * inv], axis=1).reshape(seq, 2 * quarter)
    cos = jnp.concatenate([jnp.cos(ang), jnp.cos(ang)], axis=1)
    sin = jnp.concatenate([-jnp.sin(ang), jnp.sin(ang)], axis=1)
    cos = jnp.concatenate([jnp.ones((tm, ATT_HEAD_DIM), F32), cos], axis=0)
    sin = jnp.concatenate([jnp.zeros((tm, ATT_HEAD_DIM), F32), sin], axis=0)
    return cos, sin


def _attn_bias_table():
    i = jnp.arange(CHUNK)[:, None]
    j = jnp.arange(CHUNK)[None, :]
    left = j >= i
    right = j <= i
    on = jnp.ones((CHUNK, CHUNK), bool)
    off = jnp.zeros((CHUNK, CHUNK), bool)
    kinds = [
        [off, off, off],
        [left, on, right],
        [left, on, off],
        [off, on, right],
        [off, on, off],
    ]
    masks = jnp.stack([jnp.concatenate(k, axis=1) for k in kinds])
    return jnp.where(masks, 0.0, NEG_INF).astype(F32)


def kernel(x, c, ctx, c_ctx, w_mod, b_mod, norm_w, w_ffn_in, w_ffn_out, e_w_in, e_conv_w, e_conv_b, e_dt_bias,
           e_a_log, e_d_skip, e_ssd_norm_w, e_sgu_w, e_sgu_b, e_w_out, o_w_qkv, o_sink, o_w_out):
    batch, seq, d = x.shape
    ctx_len = ctx.shape[1]
    depth = w_mod.shape[0]
    ffn_hidden = w_ffn_out.shape[1]
    inner = d // 2
    heads = inner // SSD_HEAD_DIM
    hpg = heads // SSD_GROUPS
    xbc_dim = inner + 2 * SSD_GROUPS * SSD_STATE
    sgu_width = d - inner
    sgu_groups = sgu_width // LANES
    taps = e_conv_w.shape[1]
    att_heads = d // ATT_HEAD_DIM
    kv_heads = att_heads // ATT_GROUP
    q_cols = att_heads * ATT_HEAD_DIM
    kv_cols = kv_heads * ATT_HEAD_DIM

    nc = batch * ctx_len
    nl = batch * seq
    m = nc + nl
    nct = ctx_len // CHUNK
    nlt = seq // CHUNK
    assert ctx_len % CHUNK == 0 and seq % CHUNK == 0 and seq % GRID_W == 0
    assert hpg % 2 == 0 and 2 * heads <= DT_PAD and (3 * inner) % xbc_dim == 0 and inner == sgu_width
    assert q_cols == d and taps // 2 <= HALO

    tm = _pick(math.gcd(nc, seq), 1024)
    tr = _pick(math.gcd(nc, seq), 512)
    ncb = nc // tm
    n_mod = 1 + batch
    mod_rows = 8

    def mod_row(i, t):
        return jnp.where(i < nc // t, 0, 1 + (i - nc // t) // (seq // t))

    cond = jnp.concatenate([c_ctx[None, :], c, jnp.zeros((mod_rows - n_mod, d), F32)], axis=0)
    mod = _modulation(cond, w_mod, b_mod).reshape(depth * mod_rows, 1, 6 * d)
    norm_w3 = norm_w.reshape(depth * 4, 1, d)

    def mod_spec(layer, part, blk):
        return pl.BlockSpec((None, 1, d), lambda *g: (layer * mod_rows + mod_row(blk(*g), tm), 0, part))

    def vec_spec(row):
        return pl.BlockSpec((None, 1, d), lambda *g: (row, 0, 0))

    ctb = nc // tr
    xs, h = pl.pallas_call(
        functools.partial(_stream_norm_kernel, ctx_blocks=ctb),
        grid=(m // tr,),
        in_specs=[
            pl.BlockSpec((tr, d), lambda i: (jnp.minimum(i, ctb - 1), 0)),
            pl.BlockSpec((tr, d), lambda i: (jnp.maximum(i - ctb, 0), 0)),
            vec_spec(0),
            mod_spec(0, 1, lambda i: i * tr // tm),
            mod_spec(0, 0, lambda i: i * tr // tm),
        ],
        out_specs=[pl.BlockSpec((tr, d), lambda i: (i, 0)), pl.BlockSpec((tr, d), lambda i: (i, 0))],
        out_shape=[jax.ShapeDtypeStruct((m, d), F32), jax.ShapeDtypeStruct((m, d), BF16)],
        compiler_params=_params(("parallel",)),
    )(ctx.reshape(nc, d), x.reshape(nl, d), norm_w3, mod, mod)

    def project(a, w, wl, kernel_fn, tn, extra_in=(), extra_specs=(), extra_out=(), extra_out_specs=()):
        n_out = w.shape[2]
        res = pl.pallas_call(
            kernel_fn,
            grid=(m // tm, n_out // tn),
            in_specs=[pl.BlockSpec((tm, d), lambda i, j: (i, 0)),
                      pl.BlockSpec((None, d, tn), lambda i, j: (wl, 0, j))] + list(extra_specs),
            out_specs=[pl.BlockSpec((tm, tn), lambda i, j: (i, j))] + list(extra_out_specs),
            out_shape=[jax.ShapeDtypeStruct((m, n_out), BF16)] + list(extra_out),
            compiler_params=_params(("parallel", "arbitrary")),
        )(a, w, *extra_in)
        return res if extra_out else res[0]

    def fused_resid(layer, sub, next_layer, xs_in, lat_only, wl, a_list=None, w=None, h_in=None, w_in=None,
                    w_out=None):
        mlp = a_list is None
        rows = nl if lat_only else m
        nblk = rows // tm
        mod_off = ncb if lat_only else 0
        tn = _pick(d, 512)
        nslab = d // tn
        if mlp:
            tf = _pick(ffn_hidden, 512)
            nsteps = ffn_hidden // tf
        else:
            nsteps = nslab
        ne = 1
        while ne * 2 <= min(nsteps, 8) and (tm // (ne * 2)) % 16 == 0:
            ne *= 2
        se = tm // ne
        has_next = next_layer is not None

        def off(arr):
            return (arr.shape[0] - rows) // tm

        def blk_map(arr):
            return lambda i, j: (jnp.minimum(i, nblk - 1) + off(arr), 0)

        def sub_idx(i, j):
            return jnp.where(i == 0, 0, (i - 1) * ne + jnp.minimum(j, ne - 1))

        def prev_blk(i, j):
            return jnp.maximum(i - 1, 0) + mod_off

        if mlp:
            in_specs = [
                pl.BlockSpec((tm, d), blk_map(h_in)),
                pl.BlockSpec((None, d, tf), lambda i, j: (wl, 0, j)),
                pl.BlockSpec((None, d, tf), lambda i, j: (wl, 0, j + nsteps)),
                pl.BlockSpec((None, tf, d), lambda i, j: (wl, j, 0)),
            ]
            args = [h_in, w_in, w_in, w_out]
        else:
            in_specs = [pl.BlockSpec((tm, a.shape[1]), blk_map(a)) for a in a_list]
            in_specs.append(pl.BlockSpec((None, nslab, w.shape[2], tn), lambda i, j: (wl, 0, 0, 0),
                                         pipeline_mode=pl.Buffered(1)))
            args = list(a_list) + [w]
        x_off = off(xs_in) * ne
        in_specs += [
            pl.BlockSpec((se, d), lambda i, j: (sub_idx(i, j) + x_off, 0)),
            mod_spec(layer, 2 if sub == 0 else 5, prev_blk),
            vec_spec(layer * 4 + (1 if sub == 0 else 3)),
        ]
        args += [xs_in, mod, norm_w3]
        out_specs = [pl.BlockSpec((se, d), lambda i, j: (sub_idx(i, j), 0))]
        out_shape = [jax.ShapeDtypeStruct((rows, d), F32)]
        if has_next:
            nl_, nsub = next_layer
            in_specs += [vec_spec(nl_ * 4 + (0 if nsub == 0 else 2)),
                         mod_spec(nl_, 1 if nsub == 0 else 4, prev_blk),
                         mod_spec(nl_, 0 if nsub == 0 else 3, prev_blk)]
            args += [norm_w3, mod, mod]
            out_specs.append(pl.BlockSpec((se, d), lambda i, j: (sub_idx(i, j), 0)))
            out_shape.append(jax.ShapeDtypeStruct((rows, d), BF16))
        res = pl.pallas_call(
            functools.partial(_fused_resid_kernel, mlp=mlp, n_a=0 if mlp else len(a_list), nblk=nblk, ne=ne, se=se,
                              has_next=has_next),
            grid=(nblk + 1, nsteps),
            in_specs=in_specs,
            out_specs=out_specs,
            out_shape=out_shape,
            scratch_shapes=[pltpu.VMEM((nslab, tm, tn), F32), pltpu.VMEM((nslab, tm, tn), F32)],
            compiler_params=_params(("arbitrary", "arbitrary")),
        )(*args)
        return (res[0], res[1]) if has_next else (res[0], None)

    cos_t, sin_t = _rope_tables(seq, tm)
    bias_t = _attn_bias_table()
    nctb = nc // CHUNK

    def slabs(w):
        tn = _pick(d, 512)
        return w.astype(BF16).reshape(w.shape[0], w.shape[1], d // tn, tn).transpose(0, 2, 1, 3)

    w_ffn_in_b = w_ffn_in.astype(BF16)
    w_ffn_out_b = w_ffn_out.astype(BF16)
    e_w_out_b = slabs(e_w_out)
    o_w_out_b = slabs(o_w_out)
    z_end, xbc_end, dt_end = inner, inner + xbc_dim, inner + xbc_dim + 2 * heads
    w_main_b = jnp.concatenate([e_w_in[..., :z_end], e_w_in[..., dt_end:], e_w_in[..., z_end:xbc_end]],
                               axis=-1).astype(BF16)
    w_dt_b = jnp.pad(e_w_in[..., xbc_end:dt_end], ((0, 0), (0, 0), (0, DT_PAD - 2 * heads))).astype(BF16)
    n_rot = (q_cols + kv_cols) // ATT_HEAD_DIM
    quarter = ATT_HEAD_DIM // 4
    rot_cols = o_w_qkv[..., :q_cols + kv_cols].reshape(-1, d, n_rot, 2, 2, quarter).transpose(0, 1, 2, 4, 3, 5)
    w_qkv_b = jnp.concatenate([rot_cols.reshape(-1, d, q_cols + kv_cols), o_w_qkv[..., q_cols + kv_cols:]],
                              axis=-1).astype(BF16)

    for layer in range(depth):
        i2 = layer // 2
        lat_only = layer == depth - 1
        if layer % 2 == 0:
            dt_bias = jnp.pad(e_dt_bias[i2].reshape(1, 2 * heads), ((0, 0), (0, DT_PAD - 2 * heads)))
            p, dt = project(h, w_main_b, i2, _mm_dt_kernel, _pick(math.gcd(3 * inner, xbc_dim), 1536),
                            extra_in=(w_dt_b, dt_bias),
                            extra_specs=[pl.BlockSpec((None, d, DT_PAD), lambda i, j: (i2, 0, 0)),
                                         pl.BlockSpec((1, DT_PAD), lambda i, j: (0, 0))],
                            extra_out=[jax.ShapeDtypeStruct((m, DT_PAD), F32)],
                            extra_out_specs=[pl.BlockSpec((tm, DT_PAD), lambda i, j: (i, 0))])

            cr = _pick(math.gcd(ctx_len, seq), 256)
            xcol = (3 * inner) // xbc_dim
            hb = cr // HALO
            last_hb = m // HALO - 1
            xbc = pl.pallas_call(
                functools.partial(_conv_kernel, rows=cr, taps=taps, ctx_blocks=nc // cr,
                                  ctx_per_seq=ctx_len // cr, lat_per_seq=seq // cr),
                grid=(m // cr,),
                in_specs=[
                    pl.BlockSpec((HALO, xbc_dim), lambda i: (jnp.maximum(i * hb - 1, 0), xcol)),
                    pl.BlockSpec((cr, xbc_dim), lambda i: (i, xcol)),
                    pl.BlockSpec((HALO, xbc_dim), lambda i: (jnp.minimum((i + 1) * hb, last_hb), xcol)),
                    pl.BlockSpec((taps, xbc_dim), lambda i: (0, 0)),
                    pl.BlockSpec((1, xbc_dim), lambda i: (0, 0)),
                ],
                out_specs=pl.BlockSpec((cr, xbc_dim), lambda i: (i, 0)),
                out_shape=jax.ShapeDtypeStruct((m, xbc_dim), BF16),
                scratch_shapes=[pltpu.VMEM((cr + 2 * HALO, xbc_dim), F32)],
                compiler_params=_params(("parallel",)),
            )(p, p, p, e_conv_w[i2], e_conv_b[i2].reshape(1, xbc_dim))

            sb = 2 if nct % 2 == 0 and nlt % 2 == 0 else 1
            sbr = sb * CHUNK
            nctb_, nltb_ = nct // sb, nlt // sb

            def chunk_pos(ph, cidx):
                rev_cc = jnp.where(cidx < nctb_, nctb_ - 1 - cidx, 2 * nctb_ + nltb_ - 1 - cidx)
                return jnp.where(ph == 0, rev_cc, cidx)

            def row_block(b, cc):
                return jnp.where(cc < nctb_, b * nctb_ + cc, batch * nctb_ + b * nltb_ + cc - nctb_)

            def in_map(b, ph, cidx):
                return (row_block(b, chunk_pos(ph, cidx)), 0)

            def fwd_only_map(b, ph, cidx):
                return (row_block(b, jnp.where(ph == 0, 0, cidx)), 0)

            a_log = jnp.pad(e_a_log[i2].reshape(1, 2 * heads), ((0, 0), (0, DT_PAD - 2 * heads)))
            d_skip = jnp.repeat(e_d_skip[i2], SSD_HEAD_DIM).reshape(1, inner)
            y_ssd = pl.pallas_call(
                functools.partial(_ssd_kernel, nctb=nctb_, nltb=nltb_, sb=sb, heads=heads, inner=inner),
                grid=(batch, 2, nctb_ + nltb_),
                in_specs=[
                    pl.BlockSpec((sbr, xbc_dim), in_map),
                    pl.BlockSpec((sbr, DT_PAD), in_map),
                    pl.BlockSpec((sbr, inner), fwd_only_map),
                    pl.BlockSpec((1, DT_PAD), lambda b, ph, cidx: (0, 0)),
                    pl.BlockSpec((1, inner), lambda b, ph, cidx: (0, 0)),
                    pl.BlockSpec((1, inner), lambda b, ph, cidx: (0, 0)),
                ],
                out_specs=pl.BlockSpec((sbr, inner), fwd_only_map),
                out_shape=jax.ShapeDtypeStruct((m, inner), BF16),
                scratch_shapes=[pltpu.VMEM((SSD_GROUPS, SSD_STATE, hpg * SSD_HEAD_DIM), F32),
                                pltpu.VMEM((nct + nlt, CHUNK, inner), F32)],
                compiler_params=_params(("arbitrary", "arbitrary", "arbitrary")),
            )(xbc, dt, p, a_log, d_skip, e_ssd_norm_w[i2].reshape(1, inner))

            sr = _pick(math.gcd(ctx_len, seq), 256)
            sgu_bias = jnp.repeat(e_sgu_b[i2].T, LANES, axis=1)
            y_sgu = pl.pallas_call(
                functools.partial(_sgu_kernel, chunks=sr // CHUNK, groups=sgu_groups),
                grid=(m // sr,),
                in_specs=[
                    pl.BlockSpec((sr, sgu_width), lambda i: (i, 1)),
                    pl.BlockSpec((sr, sgu_width), lambda i: (i, 2)),
                    pl.BlockSpec((sgu_groups, CHUNK, CHUNK), lambda i: (0, 0, 0)),
                    pl.BlockSpec((CHUNK, sgu_width), lambda i: (0, 0)),
                ],
                out_specs=pl.BlockSpec((sr, sgu_width), lambda i: (i, 0)),
                out_shape=jax.ShapeDtypeStruct((m, sgu_width), BF16),
                compiler_params=_params(("parallel",)),
            )(p, p, e_sgu_w[i2].astype(BF16), sgu_bias)

            xs, h = fused_resid(layer, 0, (layer, 1), xs, lat_only, i2, a_list=[y_ssd, y_sgu], w=e_w_out_b)
        else:
            tn = _pick(kv_cols, 512)
            nblk = seq // tm
            rope_specs = [pl.BlockSpec((tm, ATT_HEAD_DIM),
                                       lambda i, j: (jnp.where(i < ncb, 0, 1 + (i - ncb) % nblk), 0))] * 2
            p = project(h, w_qkv_b, i2,
                        functools.partial(_mm_rope_kernel, q_blocks=q_cols // tn, rope_blocks=(q_cols + kv_cols) // tn,
                                          scale=ATT_HEAD_DIM ** -0.5),
                        tn, extra_in=(cos_t, sin_t), extra_specs=rope_specs)

            q_off = nctb if lat_only else 0

            def qblock(i):
                i = i + q_off
                is_ctx = i < nctb
                return is_ctx, jnp.where(is_ctx, i // nct, (i - nctb) // nlt), jnp.where(is_ctx, 0, (i - nctb) % nlt)

            def win_map(shift, colblk):
                def f(i):
                    is_ctx, b, nblk_ = qblock(i)
                    nb = jnp.clip(nblk_ + shift, 0, nlt - 1)
                    return (nctb + b * nlt + nb, colblk)
                return f

            def ctx_map(colblk):
                def f(i):
                    return (qblock(i)[1], colblk)
                return f

            def kind_map(i):
                is_ctx, b, nblk_ = qblock(i)
                kind = 1 + 2 * (nblk_ == 0).astype(jnp.int32) + (nblk_ == nlt - 1).astype(jnp.int32)
                return (jnp.where(is_ctx, 0, kind), 0, 0)

            kblk = q_cols // kv_cols
            att_rows = m - q_off * CHUNK
            att = pl.pallas_call(
                functools.partial(_attn_kernel, kv_heads=kv_heads, split=1),
                grid=(att_rows // CHUNK,),
                in_specs=[
                    pl.BlockSpec(memory_space=pltpu.SMEM),
                    pl.BlockSpec((CHUNK, q_cols), lambda i: (i + q_off, 0)),
                    pl.BlockSpec((CHUNK, kv_cols), win_map(-1, kblk)),
                    pl.BlockSpec((CHUNK, kv_cols), win_map(0, kblk)),
                    pl.BlockSpec((CHUNK, kv_cols), win_map(1, kblk)),
                    pl.BlockSpec((CHUNK, kv_cols), win_map(-1, kblk + 1)),
                    pl.BlockSpec((CHUNK, kv_cols), win_map(0, kblk + 1)),
                    pl.BlockSpec((CHUNK, kv_cols), win_map(1, kblk + 1)),
                    pl.BlockSpec((ctx_len, kv_cols), ctx_map(kblk)),
                    pl.BlockSpec((ctx_len, kv_cols), ctx_map(kblk + 1)),
                    pl.BlockSpec((None, CHUNK, 3 * CHUNK), kind_map),
                ],
                out_specs=pl.BlockSpec((CHUNK, q_cols), lambda i: (i, 0)),
                out_shape=jax.ShapeDtypeStruct((att_rows, q_cols), BF16),
                compiler_params=_params(("parallel",)),
            )(o_sink[i2], p, p, p, p, p, p, p, p, p, bias_t)

            xs, h = fused_resid(layer, 0, (layer, 1), xs, lat_only, i2, a_list=[att], w=o_w_out_b)

        xs, h = fused_resid(layer, 1, (layer + 1, 0) if layer + 1 < depth else None, xs, lat_only, layer,
                            h_in=h, w_in=w_ffn_in_b, w_out=w_ffn_out_b)

    return xs.reshape(batch, seq, d)
```

```python
import functools
import math

import jax
import jax.numpy as jnp
from jax import lax
from jax.experimental import pallas as pl
from jax.experimental.pallas import tpu as pltpu

F32 = jnp.float32
BF16 = jnp.bfloat16

NORM_EPS = 1e-6
NEG_INF = -1e30
GRID_W = 64
ROPE_BASE = 10000.0
SSD_HEAD_DIM = 64
SSD_GROUPS = 2
SSD_STATE = 128
ATT_HEAD_DIM = 128
ATT_GROUP = 4
CHUNK = 128
LANES = 128
DT_PAD = LANES
HALO = 16
VMEM_LIMIT_BYTES = 56 * 1024 * 1024


def _pick(n, pref):
    best = None
    for t in range(LANES, min(n, pref) + 1, LANES):
        if n % t == 0:
            best = t
    assert best is not None, (n, pref)
    return best


def _params(sem):
    return pltpu.CompilerParams(dimension_semantics=sem, vmem_limit_bytes=VMEM_LIMIT_BYTES)


def _rms(x, w):
    return x * lax.rsqrt(jnp.mean(x * x, axis=-1, keepdims=True) + NORM_EPS) * w


def _silu(x):
    return x * jax.nn.sigmoid(x)


def _gelu_tanh(x):
    return 0.5 * x * (1.0 + jnp.tanh(math.sqrt(2.0 / math.pi) * (x + 0.044715 * (x * x * x))))


def _mod_kernel(c_ref, w_ref, b_ref, o_ref):
    s = _silu(c_ref[...]).astype(BF16)
    o_ref[0] = jnp.dot(s, w_ref[0].astype(BF16), preferred_element_type=F32) + b_ref[0]


def _modulation(cond, w_mod, b_mod):
    depth, d, n = w_mod.shape
    rows = cond.shape[0]
    tn = _pick(n, 1024)
    return pl.pallas_call(
        _mod_kernel,
        grid=(depth, n // tn),
        in_specs=[
            pl.BlockSpec((rows, d), lambda l, j: (0, 0)),
            pl.BlockSpec((1, d, tn), lambda l, j: (l, 0, j)),
            pl.BlockSpec((1, 1, tn), lambda l, j: (l, 0, j)),
        ],
        out_specs=pl.BlockSpec((1, rows, tn), lambda l, j: (l, 0, j)),
        out_shape=jax.ShapeDtypeStruct((depth, rows, n), F32),
        compiler_params=_params(("arbitrary", "arbitrary")),
    )(cond, w_mod, b_mod.reshape(depth, 1, n))


def _stream_norm_kernel(c_ref, x_ref, nw_ref, sc_ref, sh_ref, xs_ref, h_ref, *, ctx_blocks):
    xv = jnp.where(pl.program_id(0) < ctx_blocks, c_ref[...], x_ref[...])
    xs_ref[...] = xv
    h_ref[...] = (_rms(xv, nw_ref[...]) * (1.0 + sc_ref[...]) + sh_ref[...]).astype(BF16)


def _mm_dt_kernel(a_ref, w_ref, wdt_ref, b_ref, o_ref, dt_ref):
    a = a_ref[...]
    o_ref[...] = jnp.dot(a, w_ref[...], preferred_element_type=F32).astype(o_ref.dtype)

    @pl.when(pl.program_id(1) == 0)
    def _():
        p = jnp.dot(a, wdt_ref[...], preferred_element_type=F32) + b_ref[...]
        dt_ref[...] = jnp.maximum(p, 0.0) + jnp.log1p(jnp.exp(-jnp.abs(p)))


def _mm_rope_kernel(a_ref, w_ref, cos_ref, sin_ref, o_ref, *, q_blocks, rope_blocks, scale):
    j = pl.program_id(1)
    acc = jnp.dot(a_ref[...], w_ref[...], preferred_element_type=F32)
    s = jnp.where(j < q_blocks, scale, 1.0).astype(F32)
    rot = j < rope_blocks
    cos = jnp.where(rot, cos_ref[...], 1.0) * s
    sin = jnp.where(rot, sin_ref[...], 0.0) * s
    for c in range(acc.shape[1] // ATT_HEAD_DIM):
        xc = acc[:, c * ATT_HEAD_DIM:(c + 1) * ATT_HEAD_DIM]
        o_ref[:, c * ATT_HEAD_DIM:(c + 1) * ATT_HEAD_DIM] = (
            xc * cos + pltpu.roll(xc, ATT_HEAD_DIM // 2, 1) * sin).astype(o_ref.dtype)


def _unit_rms(x):
    return x * lax.rsqrt(jnp.mean(x * x, axis=-1, keepdims=True) + NORM_EPS)


def _resid_epilogue(o, x_ref, g_ref, nw_ref, nxt, xo_ref, ho_ref):
    xn = x_ref[...] + _unit_rms(o) * (g_ref[...] * nw_ref[...])
    xo_ref[...] = xn
    if ho_ref is not None:
        nwn_ref, sc_ref, sh_ref = nxt
        ho_ref[...] = (_unit_rms(xn) * (nwn_ref[...] * (1.0 + sc_ref[...])) + sh_ref[...]).astype(BF16)


def _fused_resid_kernel(*refs, mlp, n_a, nblk, ne, se, has_next):
    if mlp:
        h_ref, wg_ref, wu_ref, wo_ref = refs[:4]
        pos = 4
    else:
        a_refs = refs[:n_a]
        w_ref = refs[n_a]
        pos = n_a + 1
    x_ref, g_ref, nw_ref = refs[pos:pos + 3]
    pos += 3
    nxt = refs[pos:pos + 3] if has_next else None
    pos += 3 if has_next else 0
    xo_ref = refs[pos]
    ho_ref = refs[pos + 1] if has_next else None
    accs = refs[-2:]
    nslab, tn = accs[0].shape[0], accs[0].shape[2]
    i = pl.program_id(0)
    j = pl.program_id(1)

    def produce(acc_ref):
        if mlp:
            a = h_ref[...]
            tf = wg_ref.shape[1]
            th = tf // 2
            acts = []
            for c0 in (0, th):
                g = jnp.dot(a, wg_ref[:, c0:c0 + th], preferred_element_type=F32)
                u = jnp.dot(a, wu_ref[:, c0:c0 + th], preferred_element_type=F32)
                acts.append((_silu(g) * u).astype(BF16))
            for n in range(nslab):
                part = jnp.dot(acts[0], wo_ref[0:th, n * tn:(n + 1) * tn], preferred_element_type=F32)
                part = part + jnp.dot(acts[1], wo_ref[th:tf, n * tn:(n + 1) * tn], preferred_element_type=F32)
                acc_ref[n] = jnp.where(j == 0, 0.0, acc_ref[n]) + part
        else:
            val, k0 = None, 0
            for a_ref in a_refs:
                ka = a_ref.shape[1]
                part = jnp.dot(a_ref[...], w_ref[k0:k0 + ka, :], preferred_element_type=F32)
                val = part if val is None else val + part
                k0 += ka
            acc_ref[j] = val

    def epilogue(acc_ref):
        r0 = pl.multiple_of(jnp.minimum(j, ne - 1) * se, se)
        o = jnp.concatenate([acc_ref[n, pl.ds(r0, se), :] for n in range(nslab)], axis=1)
        _resid_epilogue(o, x_ref, g_ref, nw_ref, nxt, xo_ref, ho_ref)

    @pl.when(jnp.logical_and(i == 0, j == 0))
    def _():
        for acc_ref in accs:
            acc_ref[...] = jnp.zeros_like(acc_ref)

    @pl.when(i == 0)
    def _():
        produce(accs[0])

    for parity in (0, 1):

        @pl.when(jnp.logical_and(jnp.logical_and(i > 0, i < nblk), i % 2 == parity))
        def _(parity=parity):
            produce(accs[parity])
            epilogue(accs[1 - parity])

    @pl.when(jnp.logical_and(i == nblk, j < ne))
    def _():
        epilogue(accs[(nblk - 1) % 2])


def _conv_kernel(prev_ref, cur_ref, next_ref, w_ref, b_ref, o_ref, *, rows, taps, ctx_blocks, ctx_per_seq,
                 lat_per_seq):
    i = pl.program_id(0)
    in_ctx = i < ctx_blocks
    local = jnp.where(in_ctx, i % ctx_per_seq, (i - ctx_blocks) % lat_per_seq)
    per_seq = jnp.where(in_ctx, ctx_per_seq, lat_per_seq)
    first = local == 0
    last = local == per_seq - 1
    ext = jnp.concatenate([jnp.where(first, 0.0, prev_ref[...].astype(F32)), cur_ref[...].astype(F32),
                           jnp.where(last, 0.0, next_ref[...].astype(F32))], axis=0)
    total = rows + 2 * HALO
    acc = jnp.zeros(o_ref.shape, F32) + b_ref[...]
    for k in range(taps):
        shifted = ext if k == taps // 2 else pltpu.roll(ext, (taps // 2 - k) % total, 0)
        acc = acc + w_ref[k:k + 1, :] * shifted[HALO:HALO + rows, :]
    o_ref[...] = _silu(acc).astype(o_ref.dtype)


def _cumsum_rows(x):
    rows = x.shape[0]
    row = lax.broadcasted_iota(jnp.int32, x.shape, 0)
    k = 1
    while k < rows:
        x = x + jnp.where(row >= k, pltpu.roll(x, k, 0), 0.0)
        k *= 2
    return x


def _ssd_chunk(rev, cc, rows, xbc_ref, dt_ref, z_ref, alog_ref, dskip_ref, nw_ref, expand_ref, o_ref, state_ref,
               yacc_ref, *, heads, inner):
    q = CHUNK
    hpg = heads // SSD_GROUPS
    gw = hpg * SSD_HEAD_DIM
    n = SSD_STATE
    d = 1 if rev else 0

    dtf = dt_ref[rows, :]
    da = dtf * (-jnp.exp(alog_ref[...]))
    cs = _cumsum_rows(da)
    tot = cs[q - 1:q, :]
    acs = (tot - cs + da) if rev else cs
    acs_t = acs.T
    dt_t = dtf.T
    tot_t = acs_t[:, 0:1] if rev else acs_t[:, q - 1:q]
    w_state_t = jnp.exp(tot_t - acs_t) * dt_t
    w_rows = jnp.concatenate([jnp.exp(acs), jnp.broadcast_to(jnp.exp(tot), (8, LANES))], axis=0)
    w_hi = w_rows.astype(BF16)
    w_split = jnp.concatenate([w_hi, (w_rows - w_hi.astype(F32)).astype(BF16)], axis=1)

    li = lax.broadcasted_iota(jnp.int32, (q, q), 0)
    si = lax.broadcasted_iota(jnp.int32, (q, q), 1)
    valid = (li <= si) if rev else (li >= si)
    lane = lax.broadcasted_iota(jnp.int32, (q, LANES), 1)
    lo_mask = (lane < SSD_HEAD_DIM).astype(BF16)
    hi_mask = (lane >= SSD_HEAD_DIM).astype(BF16)

    for g in range(SSD_GROUPS):
        bg = xbc_ref[rows, inner + g * n:inner + (g + 1) * n]
        cg = xbc_ref[rows, inner + (SSD_GROUPS + g) * n:inner + (SSD_GROUPS + g + 1) * n]
        cb = lax.dot_general(cg, bg, (((1,), (1,)), ((), ())), preferred_element_type=F32)
        bg_t = bg.astype(F32).T
        s_old = state_ref[g]
        w_exp = jnp.dot(w_split, expand_ref[d * SSD_GROUPS + g], preferred_element_type=F32)
        y_off = jnp.dot(cg, s_old.astype(BF16), preferred_element_type=F32) * w_exp[0:q, :]
        s_dec = s_old * w_exp[q:q + 1, :]
        for pr in range(hpg // 2):
            h1 = g * hpg + 2 * pr
            l1 = d * heads + h1
            l2 = l1 + 1
            cols = slice(h1 * SSD_HEAD_DIM, h1 * SSD_HEAD_DIM + LANES)
            pcols = slice(pr * LANES, (pr + 1) * LANES)

            def decay_mat(l):
                seg = acs[:, l:l + 1] - acs_t[l:l + 1, :]
                return cb * jnp.exp(jnp.where(valid, seg, NEG_INF)) * dt_t[l:l + 1, :]

            lhs = jnp.concatenate([decay_mat(l1), decay_mat(l2)], axis=1).astype(BF16)
            xp = xbc_ref[rows, cols]
            rhs = jnp.concatenate([xp * lo_mask, xp * hi_mask], axis=0)
            y_diag = jnp.dot(lhs, rhs, preferred_element_type=F32)
            y_pair = y_diag + y_off[:, pcols]
            lhs_s = jnp.concatenate([bg_t * w_state_t[l1:l1 + 1, :], bg_t * w_state_t[l2:l2 + 1, :]],
                                    axis=1).astype(BF16)
            s_new = jnp.dot(lhs_s, rhs, preferred_element_type=F32)
            state_ref[g, :, pcols] = s_dec[:, pcols] + s_new
            if rev:
                yacc_ref[cc, :, cols] = y_pair
            else:
                yacc_ref[cc, :, cols] = yacc_ref[cc, :, cols] + y_pair

    if not rev:
        y = yacc_ref[cc] + xbc_ref[rows, 0:inner].astype(F32) * dskip_ref[...]
        y = y * _silu(z_ref[rows, :].astype(F32))
        for g in range(SSD_GROUPS):
            yg = y[:, g * gw:(g + 1) * gw]
            o_ref[rows, g * gw:(g + 1) * gw] = _rms(yg, nw_ref[:, g * gw:(g + 1) * gw]).astype(o_ref.dtype)


def _ssd_kernel(xbc_ref, dt_ref, z_ref, alog_ref, dskip_ref, nw_ref, expand_ref, o_ref, state_ref, yacc_ref, *,
                nctb, nltb, sb, heads, inner):
    ph = pl.program_id(1)
    c = pl.program_id(2)

    @pl.when(c == 0)
    def _():
        state_ref[...] = jnp.zeros_like(state_ref)

    rev_blk = jnp.where(c < nctb, nctb - 1 - c, 2 * nctb + nltb - 1 - c)
    for rev in (True, False):

        @pl.when(ph == (0 if rev else 1))
        def _(rev=rev):
            blk = rev_blk if rev else c
            for sub in (range(sb - 1, -1, -1) if rev else range(sb)):
                _ssd_chunk(rev, blk * sb + sub, slice(sub * CHUNK, (sub + 1) * CHUNK), xbc_ref, dt_ref, z_ref,
                           alog_ref, dskip_ref, nw_ref, expand_ref, o_ref, state_ref, yacc_ref, heads=heads,
                           inner=inner)


def _sgu_kernel(u_ref, v_ref, w_ref, bias_ref, o_ref, *, chunks, groups):
    for ch in range(chunks):
        rows = slice(ch * CHUNK, (ch + 1) * CHUNK)
        gv = _gelu_tanh(v_ref[rows, :].astype(F32))
        mu = jnp.mean(gv, axis=-1, keepdims=True)
        dv = gv - mu
        vn = (dv * lax.rsqrt(jnp.mean(dv * dv, axis=-1, keepdims=True) + NORM_EPS)).astype(BF16)
        gu = _gelu_tanh(u_ref[rows, :].astype(F32))
        for g in range(groups):
            cols = slice(g * LANES, (g + 1) * LANES)
            mixed = jnp.dot(w_ref[g], vn[:, cols], preferred_element_type=F32) + bias_ref[:, cols]
            o_ref[rows, cols] = (gu[:, cols] * mixed).astype(o_ref.dtype)


def _attn_kernel(sink_ref, q_ref, kp_ref, kc_ref, kn_ref, vp_ref, vc_ref, vn_ref, kx_ref, vx_ref, bias_ref,
                 o_ref, *, kv_heads, split):
    hd = ATT_HEAD_DIM
    qb = CHUNK
    nwin = 3 * CHUNK
    gs = ATT_GROUP // split
    bias = jnp.concatenate([bias_ref[...]] * gs, axis=0)
    row = lax.broadcasted_iota(jnp.int32, (gs * qb, 1), 0)
    nkeys = nwin + kx_ref.shape[0]
    ones = jnp.ones((nkeys, hd), BF16)

    def scores(u):
        kk, h0 = u // split, (u // split) * ATT_GROUP + (u % split) * gs
        kcols = slice(kk * hd, (kk + 1) * hd)
        k_all = jnp.concatenate([kp_ref[:, kcols], kc_ref[:, kcols], kn_ref[:, kcols], kx_ref[:, kcols]], axis=0)
        q_all = jnp.concatenate([q_ref[:, (h0 + g) * hd:(h0 + g + 1) * hd] for g in range(gs)], axis=0)
        return lax.dot_general(q_all, k_all, (((1,), (1,)), ((), ())), preferred_element_type=F32)

    def softmax(u, s):
        h0 = (u // split) * ATT_GROUP + (u % split) * gs
        s = jnp.concatenate([s[:, :nwin] + bias, s[:, nwin:]], axis=1)
        sink = jnp.full((gs * qb, 1), sink_ref[h0], F32)
        for g in range(1, gs):
            sink = jnp.where(row >= g * qb, sink_ref[h0 + g], sink)
        m = jnp.maximum(jnp.max(s, axis=-1, keepdims=True), sink)
        return jnp.exp(s - m).astype(BF16), jnp.exp(sink - m)

    def values(u, p, sink_p):
        kk, h0 = u // split, (u // split) * ATT_GROUP + (u % split) * gs
        kcols = slice(kk * hd, (kk + 1) * hd)
        v_all = jnp.concatenate([vp_ref[:, kcols], vc_ref[:, kcols], vn_ref[:, kcols], vx_ref[:, kcols]], axis=0)
        ov = jnp.dot(p, jnp.concatenate([v_all, ones], axis=1), preferred_element_type=F32)
        o = ov[:, :hd] * (1.0 / (ov[:, hd:hd + 1] + sink_p))
        for g in range(gs):
            o_ref[:, (h0 + g) * hd:(h0 + g + 1) * hd] = o[g * qb:(g + 1) * qb, :].astype(o_ref.dtype)

    units = kv_heads * split
    s_q, p_q = {}, {}
    for t in range(units + 2):
        if t < units:
            s_q[t] = scores(t)
        if 0 <= t - 2 < units:
            values(t - 2, *p_q.pop(t - 2))
        if 0 <= t - 1 < units:
            p_q[t - 1] = softmax(t - 1, s_q.pop(t - 1))


def _rope_tables(seq, tm):
    pos = jnp.arange(seq, dtype=jnp.int32)
    row = (pos // GRID_W).astype(F32)
    col = (pos % GRID_W).astype(F32)
    quarter = ATT_HEAD_DIM // 4
    inv = ROPE_BASE ** (-jnp.arange(quarter, dtype=F32) / quarter)
    ang = jnp.stack([row[:, None] * inv, col[:, None] * inv], axis=1).reshape(seq, 2 * quarter)
    cos = jnp.concatenate([jnp.cos(ang), jnp.cos(ang)], axis=1)
    sin = jnp.concatenate([-jnp.sin(ang), jnp.sin(ang)], axis=1)
    cos = jnp.concatenate([jnp.ones((tm, ATT_HEAD_DIM), F32), cos], axis=0)
    sin = jnp.concatenate([jnp.zeros((tm, ATT_HEAD_DIM), F32), sin], axis=0)
    return cos, sin


def _attn_bias_table():
    i = jnp.arange(CHUNK)[:, None]
    j = jnp.arange(CHUNK)[None, :]
    left = j >= i
    right = j <= i
    on = jnp.ones((CHUNK, CHUNK), bool)
    off = jnp.zeros((CHUNK, CHUNK), bool)
    kinds = [
        [off, off, off],
        [left, on, right],
        [left, on, off],
        [off, on, right],
        [off, on, off],
    ]
    masks = jnp.stack([jnp.concatenate(k, axis=1) for k in kinds])
    return jnp.where(masks, 0.0, NEG_INF).astype(F32)


def kernel(x, c, ctx, c_ctx, w_mod, b_mod, norm_w, w_ffn_in, w_ffn_out, e_w_in, e_conv_w, e_conv_b, e_dt_bias,
           e_a_log, e_d_skip, e_ssd_norm_w, e_sgu_w, e_sgu_b, e_w_out, o_w_qkv, o_sink, o_w_out):
    batch, seq, d = x.shape
    ctx_len = ctx.shape[1]
    depth = w_mod.shape[0]
    ffn_hidden = w_ffn_out.shape[1]
    inner = d // 2
    heads = inner // SSD_HEAD_DIM
    hpg = heads // SSD_GROUPS
    xbc_dim = inner + 2 * SSD_GROUPS * SSD_STATE
    sgu_width = d - inner
    sgu_groups = sgu_width // LANES
    taps = e_conv_w.shape[1]
    att_heads = d // ATT_HEAD_DIM
    kv_heads = att_heads // ATT_GROUP
    q_cols = att_heads * ATT_HEAD_DIM
    kv_cols = kv_heads * ATT_HEAD_DIM

    nc = batch * ctx_len
    nl = batch * seq
    m = nc + nl
    nct = ctx_len // CHUNK
    nlt = seq // CHUNK
    assert ctx_len % CHUNK == 0 and seq % CHUNK == 0 and seq % GRID_W == 0
    assert hpg % 2 == 0 and 2 * heads <= DT_PAD and (3 * inner) % xbc_dim == 0 and inner == sgu_width
    assert q_cols == d and taps // 2 <= HALO

    tm = _pick(math.gcd(nc, seq), 1024)
    tr = _pick(math.gcd(nc, seq), 512)
    ncb = nc // tm
    n_mod = 1 + batch
    mod_rows = 8

    def mod_row(i, t):
        return jnp.where(i < nc // t, 0, 1 + (i - nc // t) // (seq // t))

    cond = jnp.concatenate([c_ctx[None, :], c, jnp.zeros((mod_rows - n_mod, d), F32)], axis=0)
    mod = _modulation(cond, w_mod, b_mod).reshape(depth * mod_rows, 1, 6 * d)
    norm_w3 = norm_w.reshape(depth * 4, 1, d)

    def mod_spec(layer, part, blk):
        return pl.BlockSpec((None, 1, d), lambda *g: (layer * mod_rows + mod_row(blk(*g), tm), 0, part))

    def vec_spec(row):
        return pl.BlockSpec((None, 1, d), lambda *g: (row, 0, 0))

    ctb = nc // tr
    xs, h = pl.pallas_call(
        functools.partial(_stream_norm_kernel, ctx_blocks=ctb),
        grid=(m // tr,),
        in_specs=[
            pl.BlockSpec((tr, d), lambda i: (jnp.minimum(i, ctb - 1), 0)),
            pl.BlockSpec((tr, d), lambda i: (jnp.maximum(i - ctb, 0), 0)),
            vec_spec(0),
            mod_spec(0, 1, lambda i: i * tr // tm),
            mod_spec(0, 0, lambda i: i * tr // tm),
        ],
        out_specs=[pl.BlockSpec((tr, d), lambda i: (i, 0)), pl.BlockSpec((tr, d), lambda i: (i, 0))],
        out_shape=[jax.ShapeDtypeStruct((m, d), F32), jax.ShapeDtypeStruct((m, d), BF16)],
        compiler_params=_params(("parallel",)),
    )(ctx.reshape(nc, d), x.reshape(nl, d), norm_w3, mod, mod)

    def project(a, w, wl, kernel_fn, tn, extra_in=(), extra_specs=(), extra_out=(), extra_out_specs=()):
        n_out = w.shape[2]
        res = pl.pallas_call(
            kernel_fn,
            grid=(m // tm, n_out // tn),
            in_specs=[pl.BlockSpec((tm, d), lambda i, j: (i, 0)),
                      pl.BlockSpec((None, d, tn), lambda i, j: (wl, 0, j))] + list(extra_specs),
            out_specs=[pl.BlockSpec((tm, tn), lambda i, j: (i, j))] + list(extra_out_specs),
            out_shape=[jax.ShapeDtypeStruct((m, n_out), BF16)] + list(extra_out),
            compiler_params=_params(("parallel", "arbitrary")),
        )(a, w, *extra_in)
        return res if extra_out else res[0]

    def fused_resid(layer, sub, next_layer, xs_in, lat_only, wl, a_list=None, w=None, h_in=None, w_in=None,
                    w_out=None):
        mlp = a_list is None
        rows = nl if lat_only else m
        nblk = rows // tm
        mod_off = ncb if lat_only else 0
        tn = _pick(d, 512)
        nslab = d // tn
        if mlp:
            tf = _pick(ffn_hidden, 512)
            nsteps = ffn_hidden // tf
        else:
            nsteps = nslab
        ne = 1
        while ne * 2 <= min(nsteps, 8) and (tm // (ne * 2)) % 16 == 0:
            ne *= 2
        se = tm // ne
        has_next = next_layer is not None

        def off(arr):
            return (arr.shape[0] - rows) // tm

        def blk_map(arr):
            return lambda i, j: (jnp.minimum(i, nblk - 1) + off(arr), 0)

        def sub_idx(i, j):
            return jnp.where(i == 0, 0, (i - 1) * ne + jnp.minimum(j, ne - 1))

        def prev_blk(i, j):
            return jnp.maximum(i - 1, 0) + mod_off

        if mlp:
            in_specs = [
                pl.BlockSpec((tm, d), blk_map(h_in)),
                pl.BlockSpec((None, d, tf), lambda i, j: (wl, 0, j)),
                pl.BlockSpec((None, d, tf), lambda i, j: (wl, 0, j + nsteps)),
                pl.BlockSpec((None, tf, d), lambda i, j: (wl, j, 0)),
            ]
            args = [h_in, w_in, w_in, w_out]
        else:
            in_specs = [pl.BlockSpec((tm, a.shape[1]), blk_map(a)) for a in a_list]
            in_specs.append(pl.BlockSpec((None, w.shape[1], tn), lambda i, j: (wl, 0, j)))
            args = list(a_list) + [w]
        x_off = off(xs_in) * ne
        in_specs += [
            pl.BlockSpec((se, d), lambda i, j: (sub_idx(i, j) + x_off, 0)),
            mod_spec(layer, 2 if sub == 0 else 5, prev_blk),
            vec_spec(layer * 4 + (1 if sub == 0 else 3)),
        ]
        args += [xs_in, mod, norm_w3]
        out_specs = [pl.BlockSpec((se, d), lambda i, j: (sub_idx(i, j), 0))]
        out_shape = [jax.ShapeDtypeStruct((rows, d), F32)]
        if has_next:
            nl_, nsub = next_layer
            in_specs += [vec_spec(nl_ * 4 + (0 if nsub == 0 else 2)),
                         mod_spec(nl_, 1 if nsub == 0 else 4, prev_blk),
                         mod_spec(nl_, 0 if nsub == 0 else 3, prev_blk)]
            args += [norm_w3, mod, mod]
            out_specs.append(pl.BlockSpec((se, d), lambda i, j: (sub_idx(i, j), 0)))
            out_shape.append(jax.ShapeDtypeStruct((rows, d), BF16))
        res = pl.pallas_call(
            functools.partial(_fused_resid_kernel, mlp=mlp, n_a=0 if mlp else len(a_list), nblk=nblk, ne=ne, se=se,
                              has_next=has_next),
            grid=(nblk + 1, nsteps),
            in_specs=in_specs,
            out_specs=out_specs,
            out_shape=out_shape,
            scratch_shapes=[pltpu.VMEM((nslab, tm, tn), F32), pltpu.VMEM((nslab, tm, tn), F32)],
            compiler_params=_params(("arbitrary", "arbitrary")),
        )(*args)
        return (res[0], res[1]) if has_next else (res[0], None)

    cos_t, sin_t = _rope_tables(seq, tm)
    bias_t = _attn_bias_table()
    nctb = nc // CHUNK
    gwid = hpg * SSD_HEAD_DIM
    src_col = ((jnp.arange(2)[:, None, None] * heads + jnp.arange(SSD_GROUPS)[None, :, None] * hpg
                + jnp.arange(gwid)[None, None, :] // SSD_HEAD_DIM))
    ssd_expand = (jnp.arange(2 * LANES)[None, None, :, None] % LANES == src_col[:, :, None, :]).astype(BF16)
    ssd_expand = ssd_expand.reshape(2 * SSD_GROUPS, 2 * LANES, gwid)

    w_ffn_in_b = w_ffn_in.astype(BF16)
    w_ffn_out_b = w_ffn_out.astype(BF16)
    e_w_out_b = e_w_out.astype(BF16)
    o_w_out_b = o_w_out.astype(BF16)
    z_end, xbc_end, dt_end = inner, inner + xbc_dim, inner + xbc_dim + 2 * heads
    w_main_b = jnp.concatenate([e_w_in[..., :z_end], e_w_in[..., dt_end:], e_w_in[..., z_end:xbc_end]],
                               axis=-1).astype(BF16)
    w_dt_b = jnp.pad(e_w_in[..., xbc_end:dt_end], ((0, 0), (0, 0), (0, DT_PAD - 2 * heads))).astype(BF16)
    n_rot = (q_cols + kv_cols) // ATT_HEAD_DIM
    quarter = ATT_HEAD_DIM // 4
    rot_cols = o_w_qkv[..., :q_cols + kv_cols].reshape(-1, d, n_rot, 2, 2, quarter).transpose(0, 1, 2, 4, 3, 5)
    w_qkv_b = jnp.concatenate([rot_cols.reshape(-1, d, q_cols + kv_cols), o_w_qkv[..., q_cols + kv_cols:]],
                              axis=-1).astype(BF16)

    for layer in range(depth):
        i2 = layer // 2
        lat_only = layer == depth - 1
        if layer % 2 == 0:
            dt_bias = jnp.pad(e_dt_bias[i2].reshape(1, 2 * heads), ((0, 0), (0, DT_PAD - 2 * heads)))
            p, dt = project(h, w_main_b, i2, _mm_dt_kernel, _pick(math.gcd(3 * inner, xbc_dim), 1536),
                            extra_in=(w_dt_b, dt_bias),
                            extra_specs=[pl.BlockSpec((None, d, DT_PAD), lambda i, j: (i2, 0, 0)),
                                         pl.BlockSpec((1, DT_PAD), lambda i, j: (0, 0))],
                            extra_out=[jax.ShapeDtypeStruct((m, DT_PAD), F32)],
                            extra_out_specs=[pl.BlockSpec((tm, DT_PAD), lambda i, j: (i, 0))])

            cr = _pick(math.gcd(ctx_len, seq), 256)
            xcol = (3 * inner) // xbc_dim
            hb = cr // HALO
            last_hb = m // HALO - 1
            xbc = pl.pallas_call(
                functools.partial(_conv_kernel, rows=cr, taps=taps, ctx_blocks=nc // cr,
                                  ctx_per_seq=ctx_len // cr, lat_per_seq=seq // cr),
                grid=(m // cr,),
                in_specs=[
                    pl.BlockSpec((HALO, xbc_dim), lambda i: (jnp.maximum(i * hb - 1, 0), xcol)),
                    pl.BlockSpec((cr, xbc_dim), lambda i: (i, xcol)),
                    pl.BlockSpec((HALO, xbc_dim), lambda i: (jnp.minimum((i + 1) * hb, last_hb), xcol)),
                    pl.BlockSpec((taps, xbc_dim), lambda i: (0, 0)),
                    pl.BlockSpec((1, xbc_dim), lambda i: (0, 0)),
                ],
                out_specs=pl.BlockSpec((cr, xbc_dim), lambda i: (i, 0)),
                out_shape=jax.ShapeDtypeStruct((m, xbc_dim), BF16),
                compiler_params=_params(("parallel",)),
            )(p, p, p, e_conv_w[i2], e_conv_b[i2].reshape(1, xbc_dim))

            sb = 2 if nct % 2 == 0 and nlt % 2 == 0 else 1
            sbr = sb * CHUNK
            nctb_, nltb_ = nct // sb, nlt // sb

            def chunk_pos(ph, cidx):
                rev_cc = jnp.where(cidx < nctb_, nctb_ - 1 - cidx, 2 * nctb_ + nltb_ - 1 - cidx)
                return jnp.where(ph == 0, rev_cc, cidx)

            def row_block(b, cc):
                return jnp.where(cc < nctb_, b * nctb_ + cc, batch * nctb_ + b * nltb_ + cc - nctb_)

            def in_map(b, ph, cidx):
                return (row_block(b, chunk_pos(ph, cidx)), 0)

            def fwd_only_map(b, ph, cidx):
                return (row_block(b, jnp.where(ph == 0, 0, cidx)), 0)

            a_log = jnp.pad(e_a_log[i2].reshape(1, 2 * heads), ((0, 0), (0, DT_PAD - 2 * heads)))
            d_skip = jnp.repeat(e_d_skip[i2], SSD_HEAD_DIM).reshape(1, inner)
            y_ssd = pl.pallas_call(
                functools.partial(_ssd_kernel, nctb=nctb_, nltb=nltb_, sb=sb, heads=heads, inner=inner),
                grid=(batch, 2, nctb_ + nltb_),
                in_specs=[
                    pl.BlockSpec((sbr, xbc_dim), in_map),
                    pl.BlockSpec((sbr, DT_PAD), in_map),
                    pl.BlockSpec((sbr, inner), fwd_only_map),
                    pl.BlockSpec((1, DT_PAD), lambda b, ph, cidx: (0, 0)),
                    pl.BlockSpec((1, inner), lambda b, ph, cidx: (0, 0)),
                    pl.BlockSpec((1, inner), lambda b, ph, cidx: (0, 0)),
                    pl.BlockSpec((2 * SSD_GROUPS, 2 * LANES, hpg * SSD_HEAD_DIM), lambda b, ph, cidx: (0, 0, 0)),
                ],
                out_specs=pl.BlockSpec((sbr, inner), fwd_only_map),
                out_shape=jax.ShapeDtypeStruct((m, inner), BF16),
                scratch_shapes=[pltpu.VMEM((SSD_GROUPS, SSD_STATE, hpg * SSD_HEAD_DIM), F32),
                                pltpu.VMEM((nct + nlt, CHUNK, inner), F32)],
                compiler_params=_params(("arbitrary", "arbitrary", "arbitrary")),
            )(xbc, dt, p, a_log, d_skip, e_ssd_norm_w[i2].reshape(1, inner), ssd_expand)

            sr = _pick(math.gcd(ctx_len, seq), 256)
            sgu_bias = jnp.repeat(e_sgu_b[i2].T, LANES, axis=1)
            y_sgu = pl.pallas_call(
                functools.partial(_sgu_kernel, chunks=sr // CHUNK, groups=sgu_groups),
                grid=(m // sr,),
                in_specs=[
                    pl.BlockSpec((sr, sgu_width), lambda i: (i, 1)),
                    pl.BlockSpec((sr, sgu_width), lambda i: (i, 2)),
                    pl.BlockSpec((sgu_groups, CHUNK, CHUNK), lambda i: (0, 0, 0)),
                    pl.BlockSpec((CHUNK, sgu_width), lambda i: (0, 0)),
                ],
                out_specs=pl.BlockSpec((sr, sgu_width), lambda i: (i, 0)),
                out_shape=jax.ShapeDtypeStruct((m, sgu_width), BF16),
                compiler_params=_params(("parallel",)),
            )(p, p, e_sgu_w[i2].astype(BF16), sgu_bias)

            xs, h = fused_resid(layer, 0, (layer, 1), xs, lat_only, i2, a_list=[y_ssd, y_sgu], w=e_w_out_b)
        else:
            tn = _pick(kv_cols, 512)
            nblk = seq // tm
            rope_specs = [pl.BlockSpec((tm, ATT_HEAD_DIM),
                                       lambda i, j: (jnp.where(i < ncb, 0, 1 + (i - ncb) % nblk), 0))] * 2
            p = project(h, w_qkv_b, i2,
                        functools.partial(_mm_rope_kernel, q_blocks=q_cols // tn, rope_blocks=(q_cols + kv_cols) // tn,
                                          scale=ATT_HEAD_DIM ** -0.5),
                        tn, extra_in=(cos_t, sin_t), extra_specs=rope_specs)

            q_off = nctb if lat_only else 0

            def qblock(i):
                i = i + q_off
                is_ctx = i < nctb
                return is_ctx, jnp.where(is_ctx, i // nct, (i - nctb) // nlt), jnp.where(is_ctx, 0, (i - nctb) % nlt)

            def win_map(shift, colblk):
                def f(i):
                    is_ctx, b, nblk_ = qblock(i)
                    nb = jnp.clip(nblk_ + shift, 0, nlt - 1)
                    return (nctb + b * nlt + nb, colblk)
                return f

            def ctx_map(colblk):
                def f(i):
                    return (qblock(i)[1], colblk)
                return f

            def kind_map(i):
                is_ctx, b, nblk_ = qblock(i)
                kind = 1 + 2 * (nblk_ == 0).astype(jnp.int32) + (nblk_ == nlt - 1).astype(jnp.int32)
                return (jnp.where(is_ctx, 0, kind), 0, 0)

            kblk = q_cols // kv_cols
            att_rows = m - q_off * CHUNK
            att = pl.pallas_call(
                functools.partial(_attn_kernel, kv_heads=kv_heads, split=1),
                grid=(att_rows // CHUNK,),
                in_specs=[
                    pl.BlockSpec(memory_space=pltpu.SMEM),
                    pl.BlockSpec((CHUNK, q_cols), lambda i: (i + q_off, 0)),
                    pl.BlockSpec((CHUNK, kv_cols), win_map(-1, kblk)),
                    pl.BlockSpec((CHUNK, kv_cols), win_map(0, kblk)),
                    pl.BlockSpec((CHUNK, kv_cols), win_map(1, kblk)),
                    pl.BlockSpec((CHUNK, kv_cols), win_map(-1, kblk + 1)),
                    pl.BlockSpec((CHUNK, kv_cols), win_map(0, kblk + 1)),
                    pl.BlockSpec((CHUNK, kv_cols), win_map(1, kblk + 1)),
                    pl.BlockSpec((ctx_len, kv_cols), ctx_map(kblk)),
                    pl.BlockSpec((ctx_len, kv_cols), ctx_map(kblk + 1)),
                    pl.BlockSpec((None, CHUNK, 3 * CHUNK), kind_map),
                ],
                out_specs=pl.BlockSpec((CHUNK, q_cols), lambda i: (i, 0)),
                out_shape=jax.ShapeDtypeStruct((att_rows, q_cols), BF16),
                compiler_params=_params(("parallel",)),
            )(o_sink[i2], p, p, p, p, p, p, p, p, p, bias_t)

            xs, h = fused_resid(layer, 0, (layer, 1), xs, lat_only, i2, a_list=[att], w=o_w_out_b)

        xs, h = fused_resid(layer, 1, (layer + 1, 0) if layer + 1 < depth else None, xs, lat_only, layer,
                            h_in=h, w_in=w_ffn_in_b, w_out=w_ffn_out_b)

    return xs.reshape(batch, seq, d)
```
